```python
import jax, jax.numpy as jnp
from jax import lax
import numpy as np

D_MODEL = 1024
BATCH = 8
SEQ = 4096
DEPTH = 4

N_A_LAYERS = DEPTH // 2
N_B_LAYERS = DEPTH - N_A_LAYERS
CHUNK = 128
GMLP_WIDTH = 2 * D_MODEL
GMLP_GROUPS = 8
N_HEADS = 16
HEAD_DIM = D_MODEL // N_HEADS
D_FF = 4 * D_MODEL
Q_BLOCK = 128
MOD_INIT = 0.5
EPS = 1e-6

kernel_name = "yoco_gmlp_stickbreaking_hybrid"


def _rmsnorm(x, g):
    xf = x.astype(jnp.float32)
    y = xf * lax.rsqrt(jnp.mean(xf * xf, axis=-1, keepdims=True) + EPS)
    return (y * g.astype(jnp.float32)).astype(x.dtype)


def _modulate(h, shift, scale):
    return h * (1 + scale[:, None, :]) + shift[:, None, :]


def _gmlp_mixer(h, w_in, v_norm_g, w_s, b_s, w_out):
    B, S, _ = h.shape
    uv = jax.nn.gelu(h @ w_in, approximate=False)
    u, v = jnp.split(uv, 2, axis=-1)
    v = _rmsnorm(v, v_norm_g)
    n_chunks = S // CHUNK
    v = v.reshape(B, n_chunks, CHUNK, GMLP_GROUPS, GMLP_WIDTH // GMLP_GROUPS)
    w_causal = jnp.tril(w_s)
    z = jnp.einsum("gts,bnsgc->bntgc", w_causal, v) + b_s.T[None, None, :, :, None]
    z = z.reshape(B, S, GMLP_WIDTH)
    return (u * z) @ w_out


def _stick_breaking(q, k, v):
    S = q.shape[2]
    scale = HEAD_DIM ** -0.5
    outs = []
    for blk in range(S // Q_BLOCK):
        t0 = blk * Q_BLOCK
        n_keys = t0 + Q_BLOCK
        qb = q[:, :, t0:n_keys]
        kb = k[:, :, :n_keys]
        vb = v[:, :, :n_keys]
        z = jnp.einsum("bhtd,bhsd->bhts", qb, kb,
                       preferred_element_type=jnp.float32) * scale
        t_pos = t0 + jnp.arange(Q_BLOCK)[:, None]
        s_pos = jnp.arange(n_keys)[None, :]
        causal = s_pos < t_pos
        log_beta = jax.nn.log_sigmoid(z)
        log_keep = jnp.where(causal, jax.nn.log_sigmoid(-z), 0.0)
        between = lax.cumsum(log_keep, axis=3, reverse=True) - log_keep
        weights = jnp.where(causal, jnp.exp(log_beta + between), 0.0)
        outs.append(jnp.einsum("bhts,bhsd->bhtd", weights.astype(vb.dtype), vb))
    return jnp.concatenate(outs, axis=2)


def setup_inputs(seed: int = 0) -> dict:
    key = jax.random.key(seed)
    ks = jax.random.split(key, 20)
    D = D_MODEL
    W = GMLP_WIDTH

    def nrm(k, shape, s):
        return jax.random.normal(k, shape, jnp.float32) * s

    return {
        "x": nrm(ks[0], (BATCH, SEQ, D), 1.0),
        "c": nrm(ks[1], (BATCH, D), 1.0),
        "mod_w": nrm(ks[2], (DEPTH, D, 6 * D), MOD_INIT * D ** -0.5),
        "mod_b": nrm(ks[3], (DEPTH, 6 * D), 0.02),
        "norm_g": 1.0 + nrm(ks[4], (DEPTH, 2, D), 0.02),
        "mlp_w_up": nrm(ks[5], (DEPTH, D, D_FF), D ** -0.5),
        "mlp_w_down": nrm(ks[6], (DEPTH, D_FF, D), D_FF ** -0.5),
        "a_w_in": nrm(ks[7], (N_A_LAYERS, D, 2 * W), D ** -0.5),
        "a_v_norm_g": 1.0 + nrm(ks[8], (N_A_LAYERS, W), 0.02),
        "a_w_spatial": nrm(ks[9], (N_A_LAYERS, GMLP_GROUPS, CHUNK, CHUNK), CHUNK ** -0.5),
        "a_b_spatial": 1.0 + nrm(ks[10], (N_A_LAYERS, GMLP_GROUPS, CHUNK), 0.02),
        "a_w_out": nrm(ks[11], (N_A_LAYERS, W, D), W ** -0.5),
        "kv_mod_w": nrm(ks[12], (D, 2 * D), MOD_INIT * D ** -0.5),
        "kv_mod_b": nrm(ks[13], (2 * D,), 0.02),
        "kv_norm_g": 1.0 + nrm(ks[14], (D,), 0.02),
        "kv_w": nrm(ks[15], (D, 2 * D), D ** -0.5),
        "k_norm_g": 1.0 + nrm(ks[16], (HEAD_DIM,), 0.02),
        "b_w_q": nrm(ks[17], (N_B_LAYERS, D, D), D ** -0.5),
        "q_norm_g": 1.0 + nrm(ks[18], (N_B_LAYERS, HEAD_DIM), 0.02),
        "b_w_out": nrm(ks[19], (N_B_LAYERS, D, D), D ** -0.5),
    }


def reference(x, c, mod_w, mod_b, norm_g, mlp_w_up, mlp_w_down, a_w_in, a_v_norm_g,
              a_w_spatial, a_b_spatial, a_w_out, kv_mod_w, kv_mod_b, kv_norm_g, kv_w,
              k_norm_g, b_w_q, q_norm_g, b_w_out):
    B, S, D = x.shape
    k_shared = None
    v_shared = None
    for layer in range(DEPTH):
        mod = c @ mod_w[layer] + mod_b[layer]
        sh1, sc1, g1, sh2, sc2, g2 = jnp.split(mod, 6, axis=-1)
        h = _modulate(_rmsnorm(x, norm_g[layer, 0]), sh1, sc1)
        if layer < N_A_LAYERS:
            i = layer
            y = _gmlp_mixer(h, a_w_in[i], a_v_norm_g[i], a_w_spatial[i],
                            a_b_spatial[i], a_w_out[i])
        else:
            j = layer - N_A_LAYERS
            q = (h @ b_w_q[j]).reshape(B, S, N_HEADS, HEAD_DIM)
            q = _rmsnorm(q, q_norm_g[j]).transpose(0, 2, 1, 3)
            o = _stick_breaking(q, k_shared, v_shared)
            y = o.transpose(0, 2, 1, 3).reshape(B, S, D) @ b_w_out[j]
        x = x + g1[:, None, :] * y
        h = _modulate(_rmsnorm(x, norm_g[layer, 1]), sh2, sc2)
        y = jnp.square(jax.nn.relu(h @ mlp_w_up[layer])) @ mlp_w_down[layer]
        x = x + g2[:, None, :] * y
        if layer == N_A_LAYERS - 1:
            kv_shift, kv_scale = jnp.split(c @ kv_mod_w + kv_mod_b, 2, axis=-1)
            hk = _modulate(_rmsnorm(x, kv_norm_g), kv_shift, kv_scale)
            k_all, v_all = jnp.split(hk @ kv_w, 2, axis=-1)
            k_shared = _rmsnorm(k_all.reshape(B, S, N_HEADS, HEAD_DIM),
                                k_norm_g).transpose(0, 2, 1, 3)
            v_shared = v_all.reshape(B, S, N_HEADS, HEAD_DIM).transpose(0, 2, 1, 3)
    return x
```

```python
import functools

import jax
import jax.numpy as jnp
from jax import lax
from jax.experimental import pallas as pl
from jax.experimental.pallas import tpu as pltpu

D_MODEL = 1024
DEPTH = 4
N_A_LAYERS = DEPTH // 2
CHUNK = 128
GMLP_WIDTH = 2 * D_MODEL
GMLP_GROUPS = 8
GROUP_WIDTH = GMLP_WIDTH // GMLP_GROUPS
N_HEADS = 16
HEAD_DIM = D_MODEL // N_HEADS
D_FF = 4 * D_MODEL
EPS = 1e-6

LANES = 128
HEADS_PER_BLOCK = LANES // HEAD_DIM
VMEM_LIMIT_BYTES = 56 * 1024 * 1024

TM_DENSE = 512
TQ = 128
TK = 128
LOG_WEIGHT_FLOOR = -104.0

F32 = jnp.float32
BF16 = jnp.bfloat16


def _dot(a, b):
    return jnp.dot(a, b, preferred_element_type=F32)


def _dot_hi_lo(a_f32, b_bf16):
    hi = a_f32.astype(BF16)
    lo = (a_f32 - hi.astype(F32)).astype(BF16)
    return _dot(hi, b_bf16) + _dot(lo, b_bf16)


def _norm_modulate(x, gain, shift, scale):
    ms = jnp.mean(x * x, axis=-1, keepdims=True)
    h = x * lax.rsqrt(ms + EPS) * gain
    return h * (1.0 + scale) + shift


def _resident(shape):
    zeros = (0,) * len(shape)
    return pl.BlockSpec(shape, lambda *_: zeros, pipeline_mode=pl.Buffered(1))


def _params(n_grid_axes):
    return pltpu.CompilerParams(
        dimension_semantics=("arbitrary",) * n_grid_axes,
        vmem_limit_bytes=VMEM_LIMIT_BYTES)


def _mod_kernel(c_ref, w_ref, b_ref, o_ref):
    o_ref[...] = _dot(c_ref[...].astype(BF16), w_ref[...].astype(BF16)) + b_ref[...]


def _mod_project(c, w, b):
    n_layers, d, nd = w.shape
    n_vec = nd // d
    bsz = c.shape[0]
    out = pl.pallas_call(
        _mod_kernel,
        grid=(n_layers, n_vec),
        in_specs=[
            pl.BlockSpec((bsz, d), lambda l, n: (0, 0)),
            pl.BlockSpec((None, d, d), lambda l, n: (l, 0, n)),
            pl.BlockSpec((None, 1, d), lambda l, n: (l, 0, n)),
        ],
        out_specs=pl.BlockSpec((None, None, bsz, d), lambda l, n: (l, n, 0, 0)),
        out_shape=jax.ShapeDtypeStruct((n_layers, n_vec, bsz, d), F32),
        compiler_params=_params(2),
        name="adaln_project",
    )(c, w, b.reshape(n_layers, 1, nd))
    return out.transpose(0, 2, 1, 3)


def _gmlp_kernel(x_ref, mod_ref, ng_ref, win_ref, vg_ref, ws_ref, bs_ref,
                 wout_ref, o_ref, gated_ref, *, tm):
    x = x_ref[...]
    h = _norm_modulate(x, ng_ref[...], mod_ref[0:1, :], mod_ref[1:2, :])
    uv = _dot(h.astype(BF16), win_ref[...])
    uv = 0.5 * uv * (1.0 + lax.erf(uv * (0.5 ** 0.5)))
    u = uv[:, :GMLP_WIDTH]
    v = uv[:, GMLP_WIDTH:]
    vms = jnp.mean(v * v, axis=-1, keepdims=True)
    vb = (v * lax.rsqrt(vms + EPS) * vg_ref[...]).astype(BF16)

    row = lax.broadcasted_iota(jnp.int32, (CHUNK, CHUNK), 0)
    col = lax.broadcasted_iota(jnp.int32, (CHUNK, CHUNK), 1)
    tril = col <= row
    for g in range(GMLP_GROUPS):
        w_causal = jnp.where(tril, ws_ref[g], 0.0).astype(BF16)
        bias = bs_ref[:, g:g + 1]
        cs = slice(g * GROUP_WIDTH, (g + 1) * GROUP_WIDTH)
        for n in range(tm // CHUNK):
            rs = slice(n * CHUNK, (n + 1) * CHUNK)
            z = _dot(w_causal, vb[rs, cs]) + bias
            gated_ref[rs, cs] = (u[rs, cs] * z).astype(BF16)
    y = _dot(gated_ref[...], wout_ref[...])
    o_ref[...] = x + mod_ref[2:3, :] * y


def _gmlp_layer(x, mod, layer, norm_g, w_in, v_norm_g, w_spatial, b_spatial_t, w_out):
    bsz, seq, d = x.shape
    tm = TM_DENSE
    return pl.pallas_call(
        functools.partial(_gmlp_kernel, tm=tm),
        grid=(bsz, seq // tm),
        in_specs=[
            pl.BlockSpec((None, tm, d), lambda b, i: (b, i, 0)),
            pl.BlockSpec((None, None, 6, d), lambda b, i: (layer, b, 0, 0)),
            _resident((1, d)),
            _resident((d, 2 * GMLP_WIDTH)),
            _resident((1, GMLP_WIDTH)),
            _resident((GMLP_GROUPS, CHUNK, CHUNK)),
            _resident((CHUNK, GMLP_GROUPS)),
            _resident((GMLP_WIDTH, d)),
        ],
        out_specs=pl.BlockSpec((None, tm, d), lambda b, i: (b, i, 0)),
        out_shape=jax.ShapeDtypeStruct(x.shape, F32),
        scratch_shapes=[pltpu.VMEM((tm, GMLP_WIDTH), BF16)],
        compiler_params=_params(2),
        name="gmlp_mixer",
    )(x, mod, norm_g, w_in, v_norm_g, w_spatial, b_spatial_t, w_out)


def _mlp_kernel(x_ref, mod_ref, ng_ref, wup_ref, wdn_ref, o_ref):
    x = x_ref[...]
    h = _norm_modulate(x, ng_ref[...], mod_ref[3:4, :], mod_ref[4:5, :])
    a = jnp.maximum(_dot(h.astype(BF16), wup_ref[...]), 0.0)
    y = _dot((a * a).astype(BF16), wdn_ref[...])
    o_ref[...] = x + mod_ref[5:6, :] * y


def _mlp_layer(x, mod, layer, norm_g, w_up, w_down):
    bsz, seq, d = x.shape
    tm = TM_DENSE
    return pl.pallas_call(
        _mlp_kernel,
        grid=(bsz, seq // tm),
        in_specs=[
            pl.BlockSpec((None, tm, d), lambda b, i: (b, i, 0)),
            pl.BlockSpec((None, None, 6, d), lambda b, i: (layer, b, 0, 0)),
            _resident((1, d)),
            _resident((d, D_FF)),
            _resident((D_FF, d)),
        ],
        out_specs=pl.BlockSpec((None, tm, d), lambda b, i: (b, i, 0)),
        out_shape=jax.ShapeDtypeStruct(x.shape, F32),
        compiler_params=_params(2),
        name="relu2_mlp",
    )(x, mod, norm_g, w_up, w_down)


def _head_rmsnorm(t, gain_tiled, out_scale):
    row = lax.broadcasted_iota(jnp.int32, (LANES, LANES), 0) // HEAD_DIM
    col = lax.broadcasted_iota(jnp.int32, (LANES, LANES), 1) // HEAD_DIM
    same_head = jnp.where(row == col, 1.0, 0.0).astype(BF16)
    pieces = []
    for j in range(t.shape[1] // LANES):
        tj = t[:, j * LANES:(j + 1) * LANES]
        ms = _dot_hi_lo(tj * tj, same_head) * (1.0 / HEAD_DIM)
        gj = gain_tiled[:, j * LANES:(j + 1) * LANES]
        pieces.append(tj * lax.rsqrt(ms + EPS) * (gj * out_scale))
    return pieces


def _qproj_kernel(x_ref, mod_ref, ng_ref, wq_ref, qg_ref, q_ref):
    h = _norm_modulate(x_ref[...], ng_ref[...], mod_ref[0:1, :], mod_ref[1:2, :])
    q = _dot(h.astype(BF16), wq_ref[...])
    for j, piece in enumerate(_head_rmsnorm(q, qg_ref[...], HEAD_DIM ** -0.5)):
        q_ref[:, j * LANES:(j + 1) * LANES] = piece.astype(BF16)


def _q_project(x, mod, layer, norm_g, w_q, q_gain_tiled):
    bsz, seq, d = x.shape
    tm = TM_DENSE
    return pl.pallas_call(
        _qproj_kernel,
        grid=(bsz, seq // tm),
        in_specs=[
            pl.BlockSpec((None, tm, d), lambda b, i: (b, i, 0)),
            pl.BlockSpec((None, None, 6, d), lambda b, i: (layer, b, 0, 0)),
            _resident((1, d)),
            _resident((d, d)),
            _resident((1, d)),
        ],
        out_specs=pl.BlockSpec((None, tm, d), lambda b, i: (b, i, 0)),
        out_shape=jax.ShapeDtypeStruct(x.shape, BF16),
        compiler_params=_params(2),
        name="q_project",
    )(x, mod, norm_g, w_q, q_gain_tiled)


def _kvproj_kernel(x_ref, mod_ref, ng_ref, wkv_ref, kg_ref, k_ref, v_ref):
    h = _norm_modulate(x_ref[...], ng_ref[...], mod_ref[0:1, :], mod_ref[1:2, :])
    kv = _dot(h.astype(BF16), wkv_ref[...])
    for j, piece in enumerate(_head_rmsnorm(kv[:, :D_MODEL], kg_ref[...], 1.0)):
        k_ref[:, j * LANES:(j + 1) * LANES] = piece.astype(BF16)
    v_ref[...] = kv[:, D_MODEL:].astype(BF16)


def _kv_project(x, kvmod, norm_g, w_kv, k_gain_tiled):
    bsz, seq, d = x.shape
    tm = TM_DENSE
    return pl.pallas_call(
        _kvproj_kernel,
        grid=(bsz, seq // tm),
        in_specs=[
            pl.BlockSpec((None, tm, d), lambda b, i: (b, i, 0)),
            pl.BlockSpec((None, None, 2, d), lambda b, i: (0, b, 0, 0)),
            _resident((1, d)),
            _resident((d, 2 * d)),
            _resident((1, d)),
        ],
        out_specs=[pl.BlockSpec((None, tm, d), lambda b, i: (b, i, 0))] * 2,
        out_shape=[jax.ShapeDtypeStruct(x.shape, BF16)] * 2,
        compiler_params=_params(2),
        name="kv_project",
    )(x, kvmod, norm_g, w_kv, k_gain_tiled)


def _attn_kernel(q_ref, k_ref, v_ref, o_ref):
    seq = q_ref.shape[0]
    lane_head = lax.broadcasted_iota(jnp.int32, (TQ, LANES), 1) // HEAD_DIM
    row = lax.broadcasted_iota(jnp.int32, (TQ, TK), 0)
    col = lax.broadcasted_iota(jnp.int32, (TQ, TK), 1)
    strictly_causal = col < row
    suffix_and_ones = jnp.concatenate(
        [jnp.where(row > col, 1.0, 0.0), jnp.ones((TK, TK), F32)], axis=1).astype(BF16)

    def block(qh, j, run, acc, diagonal):
        ks = pl.ds(pl.multiple_of(j * TK, TK), TK)
        z = lax.dot_general(qh, k_ref[ks, :], (((1,), (1,)), ((), ())),
                            preferred_element_type=F32)
        softplus = jnp.log(1.0 + jnp.exp(-jnp.abs(z)))
        log_beta = jnp.minimum(z, 0.0) - softplus
        log_keep = log_beta - z
        if diagonal:
            log_keep = jnp.where(strictly_causal, log_keep, 0.0)
        sums = _dot_hi_lo(log_keep, suffix_and_ones)
        w = jnp.exp(log_beta + sums[:, :TK] + run)
        if diagonal:
            w = jnp.where(strictly_causal, w, 0.0)
        acc = acc + _dot(w.astype(BF16), v_ref[ks, :])
        return run + sums[:, TK:], acc

    def q_block(qi, carry):
        qs = pl.ds(pl.multiple_of(qi * TQ, TQ), TQ)
        q = q_ref[qs, :]
        out = jnp.zeros((TQ, LANES), F32)
        for head in range(HEADS_PER_BLOCK):
            qh = jnp.where(lane_head == head, q, jnp.zeros_like(q))
            run, acc = block(qh, qi, jnp.zeros((TQ, TK), F32),
                             jnp.zeros((TQ, LANES), F32), True)

            def cond(state):
                j, run_max, _, _ = state
                return jnp.logical_and(j >= 0, run_max > LOG_WEIGHT_FLOOR)

            def body(state):
                j, _, run, acc = state
                run, acc = block(qh, j, run, acc, False)
                return j - 1, jnp.max(run), run, acc

            _, _, _, acc = lax.while_loop(cond, body, (qi - 1, jnp.max(run), run, acc))
            out = jnp.where(lane_head == head, acc, out)
        o_ref[qs, :] = out.astype(o_ref.dtype)
        return carry

    lax.fori_loop(0, seq // TQ, q_block, 0)


def _stick_breaking_attention(q, k, v):
    bsz, seq, d = q.shape
    spec = pl.BlockSpec((None, seq, LANES), lambda b, p: (b, 0, p))
    return pl.pallas_call(
        _attn_kernel,
        grid=(bsz, d // LANES),
        in_specs=[spec, spec, spec],
        out_specs=spec,
        out_shape=jax.ShapeDtypeStruct(q.shape, BF16),
        compiler_params=_params(2),
        name="stick_breaking_attention",
    )(q, k, v)


def _oproj_kernel(x_ref, o_ref, mod_ref, wout_ref, y_ref):
    y_ref[...] = x_ref[...] + mod_ref[2:3, :] * _dot(o_ref[...], wout_ref[...])


def _out_project(x, o, mod, layer, w_out):
    bsz, seq, d = x.shape
    tm = TM_DENSE
    return pl.pallas_call(
        _oproj_kernel,
        grid=(bsz, seq // tm),
        in_specs=[
            pl.BlockSpec((None, tm, d), lambda b, i: (b, i, 0)),
            pl.BlockSpec((None, tm, d), lambda b, i: (b, i, 0)),
            pl.BlockSpec((None, None, 6, d), lambda b, i: (layer, b, 0, 0)),
            _resident((d, d)),
        ],
        out_specs=pl.BlockSpec((None, tm, d), lambda b, i: (b, i, 0)),
        out_shape=jax.ShapeDtypeStruct(x.shape, F32),
        compiler_params=_params(2),
        name="attn_out_project",
    )(x, o, mod, w_out)


def kernel(x, c, mod_w, mod_b, norm_g, mlp_w_up, mlp_w_down, a_w_in, a_v_norm_g,
           a_w_spatial, a_b_spatial, a_w_out, kv_mod_w, kv_mod_b, kv_norm_g, kv_w,
           k_norm_g, b_w_q, q_norm_g, b_w_out):
    d = x.shape[-1]
    mod = _mod_project(c, mod_w, mod_b)
    kvmod = _mod_project(c, kv_mod_w[None], kv_mod_b[None])
    k = v = None
    for layer in range(DEPTH):
        token_gain = norm_g[layer, 0].reshape(1, d)
        channel_gain = norm_g[layer, 1].reshape(1, d)
        if layer < N_A_LAYERS:
            x = _gmlp_layer(
                x, mod, layer, token_gain, a_w_in[layer].astype(BF16),
                a_v_norm_g[layer].reshape(1, GMLP_WIDTH), a_w_spatial[layer],
                a_b_spatial[layer].T, a_w_out[layer].astype(BF16))
        else:
            j = layer - N_A_LAYERS
            q = _q_project(x, mod, layer, token_gain, b_w_q[j].astype(BF16),
                           jnp.tile(q_norm_g[j], N_HEADS).reshape(1, d))
            o = _stick_breaking_attention(q, k, v)
            x = _out_project(x, o, mod, layer, b_w_out[j].astype(BF16))
        x = _mlp_layer(x, mod, layer, channel_gain, mlp_w_up[layer].astype(BF16),
                       mlp_w_down[layer].astype(BF16))
        if layer == N_A_LAYERS - 1:
            k, v = _kv_project(x, kvmod, kv_norm_g.reshape(1, d), kv_w.astype(BF16),
                               jnp.tile(k_norm_g, N_HEADS).reshape(1, d))
    return x
```

```python
import functools

import jax
import jax.numpy as jnp
from jax import lax
from jax.experimental import pallas as pl
from jax.experimental.pallas import tpu as pltpu

D_MODEL = 1024
DEPTH = 4
N_A_LAYERS = DEPTH // 2
CHUNK = 128
GMLP_WIDTH = 2 * D_MODEL
GMLP_GROUPS = 8
GROUP_WIDTH = GMLP_WIDTH // GMLP_GROUPS
N_HEADS = 16
HEAD_DIM = D_MODEL // N_HEADS
D_FF = 4 * D_MODEL
EPS = 1e-6

LANES = 128
HEADS_PER_BLOCK = LANES // HEAD_DIM
VMEM_LIMIT_BYTES = 56 * 1024 * 1024

TM_DENSE = 512
TQ = 256
TK = 256
LOG_WEIGHT_FLOOR = -104.0

F32 = jnp.float32
BF16 = jnp.bfloat16


def _dot(a, b):
    return jnp.dot(a, b, preferred_element_type=F32)


def _dot_hi_lo(a_f32, b_bf16):
    hi = a_f32.astype(BF16)
    lo = (a_f32 - hi.astype(F32)).astype(BF16)
    return _dot(hi, b_bf16) + _dot(lo, b_bf16)


def _norm_modulate(x, gain, shift, scale):
    ms = jnp.mean(x * x, axis=-1, keepdims=True)
    h = x * lax.rsqrt(ms + EPS) * gain
    return h * (1.0 + scale) + shift


def _resident(shape):
    zeros = (0,) * len(shape)
    return pl.BlockSpec(shape, lambda *_: zeros, pipeline_mode=pl.Buffered(1))


def _params(n_grid_axes):
    return pltpu.CompilerParams(
        dimension_semantics=("arbitrary",) * n_grid_axes,
        vmem_limit_bytes=VMEM_LIMIT_BYTES)


def _mod_kernel(c_ref, w_ref, b_ref, o_ref):
    o_ref[...] = _dot(c_ref[...].astype(BF16), w_ref[...].astype(BF16)) + b_ref[...]


def _mod_project(c, w, b):
    n_layers, d, nd = w.shape
    n_vec = nd // d
    bsz = c.shape[0]
    out = pl.pallas_call(
        _mod_kernel,
        grid=(n_layers, n_vec),
        in_specs=[
            pl.BlockSpec((bsz, d), lambda l, n: (0, 0)),
            pl.BlockSpec((None, d, d), lambda l, n: (l, 0, n)),
            pl.BlockSpec((None, 1, d), lambda l, n: (l, 0, n)),
        ],
        out_specs=pl.BlockSpec((None, None, bsz, d), lambda l, n: (l, n, 0, 0)),
        out_shape=jax.ShapeDtypeStruct((n_layers, n_vec, bsz, d), F32),
        compiler_params=_params(2),
        name="adaln_project",
    )(c, w, b.reshape(n_layers, 1, nd))
    return out.transpose(0, 2, 1, 3)


def _gmlp_kernel(x_ref, mod_ref, ng_ref, win_ref, vg_ref, ws_ref, bs_ref,
                 wout_ref, o_ref, gated_ref, *, tm):
    x = x_ref[...]
    h = _norm_modulate(x, ng_ref[...], mod_ref[0:1, :], mod_ref[1:2, :])
    uv = _dot(h.astype(BF16), win_ref[...])
    uv = 0.5 * uv * (1.0 + lax.erf(uv * (0.5 ** 0.5)))
    u = uv[:, :GMLP_WIDTH]
    v = uv[:, GMLP_WIDTH:]
    vms = jnp.mean(v * v, axis=-1, keepdims=True)
    vb = (v * lax.rsqrt(vms + EPS) * vg_ref[...]).astype(BF16)

    row = lax.broadcasted_iota(jnp.int32, (CHUNK, CHUNK), 0)
    col = lax.broadcasted_iota(jnp.int32, (CHUNK, CHUNK), 1)
    tril = col <= row
    for g in range(GMLP_GROUPS):
        w_causal = jnp.where(tril, ws_ref[g], 0.0).astype(BF16)
        bias = bs_ref[:, g:g + 1]
        cs = slice(g * GROUP_WIDTH, (g + 1) * GROUP_WIDTH)
        for n in range(tm // CHUNK):
            rs = slice(n * CHUNK, (n + 1) * CHUNK)
            z = _dot(w_causal, vb[rs, cs]) + bias
            gated_ref[rs, cs] = (u[rs, cs] * z).astype(BF16)
    y = _dot(gated_ref[...], wout_ref[...])
    o_ref[...] = x + mod_ref[2:3, :] * y


def _gmlp_layer(x, mod, layer, norm_g, w_in, v_norm_g, w_spatial, b_spatial_t, w_out):
    bsz, seq, d = x.shape
    tm = TM_DENSE
    return pl.pallas_call(
        functools.partial(_gmlp_kernel, tm=tm),
        grid=(bsz, seq // tm),
        in_specs=[
            pl.BlockSpec((None, tm, d), lambda b, i: (b, i, 0)),
            pl.BlockSpec((None, None, 6, d), lambda b, i: (layer, b, 0, 0)),
            _resident((1, d)),
            _resident((d, 2 * GMLP_WIDTH)),
            _resident((1, GMLP_WIDTH)),
            _resident((GMLP_GROUPS, CHUNK, CHUNK)),
            _resident((CHUNK, GMLP_GROUPS)),
            _resident((GMLP_WIDTH, d)),
        ],
        out_specs=pl.BlockSpec((None, tm, d), lambda b, i: (b, i, 0)),
        out_shape=jax.ShapeDtypeStruct(x.shape, F32),
        scratch_shapes=[pltpu.VMEM((tm, GMLP_WIDTH), BF16)],
        compiler_params=_params(2),
        name="gmlp_mixer",
    )(x, mod, norm_g, w_in, v_norm_g, w_spatial, b_spatial_t, w_out)


def _mlp_kernel(x_ref, mod_ref, ng_ref, wup_ref, wdn_ref, o_ref):
    x = x_ref[...]
    h = _norm_modulate(x, ng_ref[...], mod_ref[3:4, :], mod_ref[4:5, :])
    a = jnp.maximum(_dot(h.astype(BF16), wup_ref[...]), 0.0)
    y = _dot((a * a).astype(BF16), wdn_ref[...])
    o_ref[...] = x + mod_ref[5:6, :] * y


def _mlp_layer(x, mod, layer, norm_g, w_up, w_down):
    bsz, seq, d = x.shape
    tm = TM_DENSE
    return pl.pallas_call(
        _mlp_kernel,
        grid=(bsz, seq // tm),
        in_specs=[
            pl.BlockSpec((None, tm, d), lambda b, i: (b, i, 0)),
            pl.BlockSpec((None, None, 6, d), lambda b, i: (layer, b, 0, 0)),
            _resident((1, d)),
            _resident((d, D_FF)),
            _resident((D_FF, d)),
        ],
        out_specs=pl.BlockSpec((None, tm, d), lambda b, i: (b, i, 0)),
        out_shape=jax.ShapeDtypeStruct(x.shape, F32),
        compiler_params=_params(2),
        name="relu2_mlp",
    )(x, mod, norm_g, w_up, w_down)


def _head_rmsnorm(t, gain_tiled, out_scale):
    row = lax.broadcasted_iota(jnp.int32, (LANES, LANES), 0) // HEAD_DIM
    col = lax.broadcasted_iota(jnp.int32, (LANES, LANES), 1) // HEAD_DIM
    same_head = jnp.where(row == col, 1.0, 0.0).astype(BF16)
    pieces = []
    for j in range(t.shape[1] // LANES):
        tj = t[:, j * LANES:(j + 1) * LANES]
        ms = _dot_hi_lo(tj * tj, same_head) * (1.0 / HEAD_DIM)
        gj = gain_tiled[:, j * LANES:(j + 1) * LANES]
        pieces.append(tj * lax.rsqrt(ms + EPS) * (gj * out_scale))
    return pieces


def _qproj_kernel(x_ref, mod_ref, ng_ref, wq_ref, qg_ref, q_ref):
    h = _norm_modulate(x_ref[...], ng_ref[...], mod_ref[0:1, :], mod_ref[1:2, :])
    q = _dot(h.astype(BF16), wq_ref[...])
    for j, piece in enumerate(_head_rmsnorm(q, qg_ref[...], HEAD_DIM ** -0.5)):
        q_ref[:, j * LANES:(j + 1) * LANES] = piece.astype(BF16)


def _q_project(x, mod, layer, norm_g, w_q, q_gain_tiled):
    bsz, seq, d = x.shape
    tm = TM_DENSE
    return pl.pallas_call(
        _qproj_kernel,
        grid=(bsz, seq // tm),
        in_specs=[
            pl.BlockSpec((None, tm, d), lambda b, i: (b, i, 0)),
            pl.BlockSpec((None, None, 6, d), lambda b, i: (layer, b, 0, 0)),
            _resident((1, d)),
            _resident((d, d)),
            _resident((1, d)),
        ],
        out_specs=pl.BlockSpec((None, tm, d), lambda b, i: (b, i, 0)),
        out_shape=jax.ShapeDtypeStruct(x.shape, BF16),
        compiler_params=_params(2),
        name="q_project",
    )(x, mod, norm_g, w_q, q_gain_tiled)


def _kvproj_kernel(x_ref, mod_ref, ng_ref, wkv_ref, kg_ref, k_ref, v_ref):
    h = _norm_modulate(x_ref[...], ng_ref[...], mod_ref[0:1, :], mod_ref[1:2, :])
    kv = _dot(h.astype(BF16), wkv_ref[...])
    for j, piece in enumerate(_head_rmsnorm(kv[:, :D_MODEL], kg_ref[...], 1.0)):
        k_ref[:, j * LANES:(j + 1) * LANES] = piece.astype(BF16)
    v_ref[...] = kv[:, D_MODEL:].astype(BF16)


def _kv_project(x, kvmod, norm_g, w_kv, k_gain_tiled):
    bsz, seq, d = x.shape
    tm = TM_DENSE
    return pl.pallas_call(
        _kvproj_kernel,
        grid=(bsz, seq // tm),
        in_specs=[
            pl.BlockSpec((None, tm, d), lambda b, i: (b, i, 0)),
            pl.BlockSpec((None, None, 2, d), lambda b, i: (0, b, 0, 0)),
            _resident((1, d)),
            _resident((d, 2 * d)),
            _resident((1, d)),
        ],
        out_specs=[pl.BlockSpec((None, tm, d), lambda b, i: (b, i, 0))] * 2,
        out_shape=[jax.ShapeDtypeStruct(x.shape, BF16)] * 2,
        compiler_params=_params(2),
        name="kv_project",
    )(x, kvmod, norm_g, w_kv, k_gain_tiled)


def _attn_kernel(q_ref, k_ref, v_ref, o_ref):
    seq = q_ref.shape[0]
    lane_head = lax.broadcasted_iota(jnp.int32, (TQ, LANES), 1) // HEAD_DIM
    row = lax.broadcasted_iota(jnp.int32, (TQ, TK), 0)
    col = lax.broadcasted_iota(jnp.int32, (TQ, TK), 1)
    strictly_causal = col < row
    suffix = jnp.where(row > col, 1.0, 0.0).astype(BF16)

    def block(qh, ks, run, acc, diagonal):
        z = lax.dot_general(qh, k_ref[ks, :], (((1,), (1,)), ((), ())),
                            preferred_element_type=F32)
        softplus = jnp.log(1.0 + jnp.exp(-jnp.abs(z)))
        log_beta = jnp.minimum(z, 0.0) - softplus
        log_keep = log_beta - z
        if diagonal:
            log_keep = jnp.where(strictly_causal, log_keep, 0.0)
        between = _dot_hi_lo(log_keep, suffix)
        w = jnp.exp(log_beta + between + run)
        if diagonal:
            w = jnp.where(strictly_causal, w, 0.0)
        acc = acc + _dot(w.astype(BF16), v_ref[ks, :])
        return run + jnp.sum(log_keep, axis=-1, keepdims=True), acc

    def q_block(qi, carry):
        qs = pl.ds(pl.multiple_of(qi * TQ, TQ), TQ)
        q = q_ref[qs, :]
        q0 = jnp.where(lane_head == 0, q, jnp.zeros_like(q))
        q1 = jnp.where(lane_head == 1, q, jnp.zeros_like(q))
        run_init = jnp.zeros((TQ, 1), F32)
        acc_init = jnp.zeros((TQ, LANES), F32)
        run0, acc0 = block(q0, qs, run_init, acc_init, True)
        run1, acc1 = block(q1, qs, run_init, acc_init, True)

        def cond(state):
            j, run_max = state[0], state[1]
            return jnp.logical_and(j >= 0, run_max > LOG_WEIGHT_FLOOR)

        def body(state):
            j, _, run0, acc0, run1, acc1 = state
            ks = pl.ds(pl.multiple_of(j * TK, TK), TK)
            run0, acc0 = block(q0, ks, run0, acc0, False)
            run1, acc1 = block(q1, ks, run1, acc1, False)
            run_max = jnp.maximum(jnp.max(run0), jnp.max(run1))
            return j - 1, run_max, run0, acc0, run1, acc1

        run_max = jnp.maximum(jnp.max(run0), jnp.max(run1))
        state = lax.while_loop(cond, body, (qi - 1, run_max, run0, acc0, run1, acc1))
        out = jnp.where(lane_head == 0, state[3], state[5])
        o_ref[qs, :] = out.astype(o_ref.dtype)
        return carry

    lax.fori_loop(0, seq // TQ, q_block, 0)


def _stick_breaking_attention(q, k, v):
    bsz, seq, d = q.shape
    spec = pl.BlockSpec((None, seq, LANES), lambda b, p: (b, 0, p))
    return pl.pallas_call(
        _attn_kernel,
        grid=(bsz, d // LANES),
        in_specs=[spec, spec, spec],
        out_specs=spec,
        out_shape=jax.ShapeDtypeStruct(q.shape, BF16),
        compiler_params=_params(2),
        name="stick_breaking_attention",
    )(q, k, v)


def _oproj_kernel(x_ref, o_ref, mod_ref, wout_ref, y_ref):
    y_ref[...] = x_ref[...] + mod_ref[2:3, :] * _dot(o_ref[...], wout_ref[...])


def _out_project(x, o, mod, layer, w_out):
    bsz, seq, d = x.shape
    tm = TM_DENSE
    return pl.pallas_call(
        _oproj_kernel,
        grid=(bsz, seq // tm),
        in_specs=[
            pl.BlockSpec((None, tm, d), lambda b, i: (b, i, 0)),
            pl.BlockSpec((None, tm, d), lambda b, i: (b, i, 0)),
            pl.BlockSpec((None, None, 6, d), lambda b, i: (layer, b, 0, 0)),
            _resident((d, d)),
        ],
        out_specs=pl.BlockSpec((None, tm, d), lambda b, i: (b, i, 0)),
        out_shape=jax.ShapeDtypeStruct(x.shape, F32),
        compiler_params=_params(2),
        name="attn_out_project",
    )(x, o, mod, w_out)


def kernel(x, c, mod_w, mod_b, norm_g, mlp_w_up, mlp_w_down, a_w_in, a_v_norm_g,
           a_w_spatial, a_b_spatial, a_w_out, kv_mod_w, kv_mod_b, kv_norm_g, kv_w,
           k_norm_g, b_w_q, q_norm_g, b_w_out):
    d = x.shape[-1]
    mod = _mod_project(c, mod_w, mod_b)
    kvmod = _mod_project(c, kv_mod_w[None], kv_mod_b[None])
    k = v = None
    for layer in range(DEPTH):
        token_gain = norm_g[layer, 0].reshape(1, d)
        channel_gain = norm_g[layer, 1].reshape(1, d)
        if layer < N_A_LAYERS:
            x = _gmlp_layer(
                x, mod, layer, token_gain, a_w_in[layer].astype(BF16),
                a_v_norm_g[layer].reshape(1, GMLP_WIDTH), a_w_spatial[layer],
                a_b_spatial[layer].T, a_w_out[layer].astype(BF16))
        else:
            j = layer - N_A_LAYERS
            q = _q_project(x, mod, layer, token_gain, b_w_q[j].astype(BF16),
                           jnp.tile(q_norm_g[j], N_HEADS).reshape(1, d))
            o = _stick_breaking_attention(q, k, v)
            x = _out_project(x, o, mod, layer, b_w_out[j].astype(BF16))
        x = _mlp_layer(x, mod, layer, channel_gain, mlp_w_up[layer].astype(BF16),
                       mlp_w_down[layer].astype(BF16))
        if layer == N_A_LAYERS - 1:
            k, v = _kv_project(x, kvmod, kv_norm_g.reshape(1, d), kv_w.astype(BF16),
                               jnp.tile(k_norm_g, N_HEADS).reshape(1, d))
    return x
```

```python
import functools

import jax
import jax.numpy as jnp
from jax import lax
from jax.experimental import pallas as pl
from jax.experimental.pallas import tpu as pltpu

D_MODEL = 1024
DEPTH = 4
N_A_LAYERS = DEPTH // 2
CHUNK = 128
GMLP_WIDTH = 2 * D_MODEL
GMLP_GROUPS = 8
GROUP_WIDTH = GMLP_WIDTH // GMLP_GROUPS
N_HEADS = 16
HEAD_DIM = D_MODEL // N_HEADS
D_FF = 4 * D_MODEL
EPS = 1e-6

LANES = 128
HEADS_PER_BLOCK = LANES // HEAD_DIM
VMEM_LIMIT_BYTES = 56 * 1024 * 1024

TM_DENSE = 512
TQ = 256
TK = TQ
LOG_WEIGHT_FLOOR = -104.0

F32 = jnp.float32
BF16 = jnp.bfloat16


def _dot(a, b):
    return jnp.dot(a, b, preferred_element_type=F32)


def _dot_hi_lo(a_f32, b_bf16):
    hi = a_f32.astype(BF16)
    lo = (a_f32 - hi.astype(F32)).astype(BF16)
    return _dot(hi, b_bf16) + _dot(lo, b_bf16)


def _norm_modulate(x, gain, shift, scale):
    ms = jnp.mean(x * x, axis=-1, keepdims=True)
    h = x * lax.rsqrt(ms + EPS) * gain
    return h * (1.0 + scale) + shift


def _resident(shape):
    zeros = (0,) * len(shape)
    return pl.BlockSpec(shape, lambda *_: zeros, pipeline_mode=pl.Buffered(1))


def _params(n_grid_axes):
    return pltpu.CompilerParams(
        dimension_semantics=("arbitrary",) * n_grid_axes,
        vmem_limit_bytes=VMEM_LIMIT_BYTES)


def _mod_kernel(c_ref, w_ref, b_ref, o_ref):
    o_ref[...] = _dot(c_ref[...].astype(BF16), w_ref[...].astype(BF16)) + b_ref[...]


def _mod_project(c, w, b):
    n_layers, d, nd = w.shape
    n_vec = nd // d
    bsz = c.shape[0]
    out = pl.pallas_call(
        _mod_kernel,
        grid=(n_layers, n_vec),
        in_specs=[
            pl.BlockSpec((bsz, d), lambda l, n: (0, 0)),
            pl.BlockSpec((None, d, d), lambda l, n: (l, 0, n)),
            pl.BlockSpec((None, 1, d), lambda l, n: (l, 0, n)),
        ],
        out_specs=pl.BlockSpec((None, None, bsz, d), lambda l, n: (l, n, 0, 0)),
        out_shape=jax.ShapeDtypeStruct((n_layers, n_vec, bsz, d), F32),
        compiler_params=_params(2),
        name="adaln_project",
    )(c, w, b.reshape(n_layers, 1, nd))
    return out.transpose(0, 2, 1, 3)


def _gmlp_kernel(x_ref, mod_ref, ng_ref, win_ref, vg_ref, ws_ref, bs_ref,
                 wout_ref, o_ref, gated_ref, *, tm):
    x = x_ref[...]
    h = _norm_modulate(x, ng_ref[...], mod_ref[0:1, :], mod_ref[1:2, :])
    uv = _dot(h.astype(BF16), win_ref[...])
    uv = 0.5 * uv * (1.0 + lax.erf(uv * (0.5 ** 0.5)))
    u = uv[:, :GMLP_WIDTH]
    v = uv[:, GMLP_WIDTH:]
    vms = jnp.mean(v * v, axis=-1, keepdims=True)
    vb = (v * lax.rsqrt(vms + EPS) * vg_ref[...]).astype(BF16)

    row = lax.broadcasted_iota(jnp.int32, (CHUNK, CHUNK), 0)
    col = lax.broadcasted_iota(jnp.int32, (CHUNK, CHUNK), 1)
    tril = col <= row
    for g in range(GMLP_GROUPS):
        w_causal = jnp.where(tril, ws_ref[g], 0.0).astype(BF16)
        bias = bs_ref[:, g:g + 1]
        cs = slice(g * GROUP_WIDTH, (g + 1) * GROUP_WIDTH)
        for n in range(tm // CHUNK):
            rs = slice(n * CHUNK, (n + 1) * CHUNK)
            z = _dot(w_causal, vb[rs, cs]) + bias
            gated_ref[rs, cs] = (u[rs, cs] * z).astype(BF16)
    y = _dot(gated_ref[...], wout_ref[...])
    o_ref[...] = x + mod_ref[2:3, :] * y


def _gmlp_layer(x, mod, layer, norm_g, w_in, v_norm_g, w_spatial, b_spatial_t, w_out):
    bsz, seq, d = x.shape
    tm = TM_DENSE
    return pl.pallas_call(
        functools.partial(_gmlp_kernel, tm=tm),
        grid=(bsz, seq // tm),
        in_specs=[
            pl.BlockSpec((None, tm, d), lambda b, i: (b, i, 0)),
            pl.BlockSpec((None, None, 6, d), lambda b, i: (layer, b, 0, 0)),
            _resident((1, d)),
            _resident((d, 2 * GMLP_WIDTH)),
            _resident((1, GMLP_WIDTH)),
            _resident((GMLP_GROUPS, CHUNK, CHUNK)),
            _resident((CHUNK, GMLP_GROUPS)),
            _resident((GMLP_WIDTH, d)),
        ],
        out_specs=pl.BlockSpec((None, tm, d), lambda b, i: (b, i, 0)),
        out_shape=jax.ShapeDtypeStruct(x.shape, F32),
        scratch_shapes=[pltpu.VMEM((tm, GMLP_WIDTH), BF16)],
        compiler_params=_params(2),
        name="gmlp_mixer",
    )(x, mod, norm_g, w_in, v_norm_g, w_spatial, b_spatial_t, w_out)


def _mlp_kernel(x_ref, mod_ref, ng_ref, wup_ref, wdn_ref, o_ref):
    x = x_ref[...]
    h = _norm_modulate(x, ng_ref[...], mod_ref[3:4, :], mod_ref[4:5, :])
    a = jnp.maximum(_dot(h.astype(BF16), wup_ref[...]), 0.0)
    y = _dot((a * a).astype(BF16), wdn_ref[...])
    o_ref[...] = x + mod_ref[5:6, :] * y


def _mlp_layer(x, mod, layer, norm_g, w_up, w_down):
    bsz, seq, d = x.shape
    tm = TM_DENSE
    return pl.pallas_call(
        _mlp_kernel,
        grid=(bsz, seq // tm),
        in_specs=[
            pl.BlockSpec((None, tm, d), lambda b, i: (b, i, 0)),
            pl.BlockSpec((None, None, 6, d), lambda b, i: (layer, b, 0, 0)),
            _resident((1, d)),
            _resident((d, D_FF)),
            _resident((D_FF, d)),
        ],
        out_specs=pl.BlockSpec((None, tm, d), lambda b, i: (b, i, 0)),
        out_shape=jax.ShapeDtypeStruct(x.shape, F32),
        compiler_params=_params(2),
        name="relu2_mlp",
    )(x, mod, norm_g, w_up, w_down)


def _head_rmsnorm(t, gain_tiled, out_scale):
    row = lax.broadcasted_iota(jnp.int32, (LANES, LANES), 0) // HEAD_DIM
    col = lax.broadcasted_iota(jnp.int32, (LANES, LANES), 1) // HEAD_DIM
    same_head = jnp.where(row == col, 1.0, 0.0).astype(BF16)
    pieces = []
    for j in range(t.shape[1] // LANES):
        tj = t[:, j * LANES:(j + 1) * LANES]
        ms = _dot_hi_lo(tj * tj, same_head) * (1.0 / HEAD_DIM)
        gj = gain_tiled[:, j * LANES:(j + 1) * LANES]
        pieces.append(tj * lax.rsqrt(ms + EPS) * (gj * out_scale))
    return pieces


def _qproj_kernel(x_ref, mod_ref, ng_ref, wq_ref, qg_ref, q_ref):
    h = _norm_modulate(x_ref[...], ng_ref[...], mod_ref[0:1, :], mod_ref[1:2, :])
    q = _dot(h.astype(BF16), wq_ref[...])
    for j, piece in enumerate(_head_rmsnorm(q, qg_ref[...], HEAD_DIM ** -0.5)):
        q_ref[:, j * LANES:(j + 1) * LANES] = piece.astype(BF16)


def _q_project(x, mod, layer, norm_g, w_q, q_gain_tiled):
    bsz, seq, d = x.shape
    tm = TM_DENSE
    return pl.pallas_call(
        _qproj_kernel,
        grid=(bsz, seq // tm),
        in_specs=[
            pl.BlockSpec((None, tm, d), lambda b, i: (b, i, 0)),
            pl.BlockSpec((None, None, 6, d), lambda b, i: (layer, b, 0, 0)),
            _resident((1, d)),
            _resident((d, d)),
            _resident((1, d)),
        ],
        out_specs=pl.BlockSpec((None, tm, d), lambda b, i: (b, i, 0)),
        out_shape=jax.ShapeDtypeStruct(x.shape, BF16),
        compiler_params=_params(2),
        name="q_project",
    )(x, mod, norm_g, w_q, q_gain_tiled)


def _kvproj_kernel(x_ref, mod_ref, ng_ref, wkv_ref, kg_ref, k_ref, v_ref):
    h = _norm_modulate(x_ref[...], ng_ref[...], mod_ref[0:1, :], mod_ref[1:2, :])
    kv = _dot(h.astype(BF16), wkv_ref[...])
    for j, piece in enumerate(_head_rmsnorm(kv[:, :D_MODEL], kg_ref[...], 1.0)):
        k_ref[:, j * LANES:(j + 1) * LANES] = piece.astype(BF16)
    v_ref[...] = kv[:, D_MODEL:].astype(BF16)


def _kv_project(x, kvmod, norm_g, w_kv, k_gain_tiled):
    bsz, seq, d = x.shape
    tm = TM_DENSE
    return pl.pallas_call(
        _kvproj_kernel,
        grid=(bsz, seq // tm),
        in_specs=[
            pl.BlockSpec((None, tm, d), lambda b, i: (b, i, 0)),
            pl.BlockSpec((None, None, 2, d), lambda b, i: (0, b, 0, 0)),
            _resident((1, d)),
            _resident((d, 2 * d)),
            _resident((1, d)),
        ],
        out_specs=[pl.BlockSpec((None, tm, d), lambda b, i: (b, i, 0))] * 2,
        out_shape=[jax.ShapeDtypeStruct(x.shape, BF16)] * 2,
        compiler_params=_params(2),
        name="kv_project",
    )(x, kvmod, norm_g, w_kv, k_gain_tiled)


def _attn_kernel(q_ref, k_ref, v_ref, o_ref):
    seq = q_ref.shape[0]
    lane_head = lax.broadcasted_iota(jnp.int32, (TQ, LANES), 1) // HEAD_DIM
    row = lax.broadcasted_iota(jnp.int32, (TQ, TK), 0)
    col = lax.broadcasted_iota(jnp.int32, (TQ, TK), 1)
    strictly_causal = col < row
    suffix = jnp.where(row > col, 1.0, 0.0).astype(BF16)

    def logits(qh, kb):
        return lax.dot_general(qh, kb, (((1,), (1,)), ((), ())),
                               preferred_element_type=F32)

    def log_terms(z, diagonal):
        softplus = jnp.log(1.0 + jnp.exp(-jnp.abs(z)))
        log_beta = jnp.minimum(z, 0.0) - softplus
        log_keep = log_beta - z
        if diagonal:
            log_keep = jnp.where(strictly_causal, log_keep, 0.0)
        between = _dot_hi_lo(log_keep, suffix)
        return log_beta, between, jnp.sum(log_keep, axis=-1, keepdims=True)

    def weights(log_beta, between, run, diagonal):
        w = jnp.exp(log_beta + between + run)
        if diagonal:
            w = jnp.where(strictly_causal, w, 0.0)
        return w.astype(BF16)

    def walk(q_heads, blocks, runs, accs):
        kbs = [k_ref[ks, :] for ks, _ in blocks]
        vbs = [v_ref[ks, :] for ks, _ in blocks]
        zs = [[logits(qh, kb) for qh in q_heads] for kb in kbs]
        terms = [[log_terms(zs[b][h], diagonal) for h in range(HEADS_PER_BLOCK)]
                 for b, (_, diagonal) in enumerate(blocks)]
        runs, accs = list(runs), list(accs)
        for b, (_, diagonal) in enumerate(blocks):
            for h in range(HEADS_PER_BLOCK):
                log_beta, between, row_sum = terms[b][h]
                w = weights(log_beta, between, runs[h], diagonal)
                accs[h] = accs[h] + _dot(w, vbs[b])
                runs[h] = runs[h] + row_sum
        return runs, accs

    def split_heads(qs):
        q = q_ref[qs, :]
        return [jnp.where(lane_head == h, q, jnp.zeros_like(q)) for h in range(HEADS_PER_BLOCK)]

    def finish(qs, q_heads, runs, accs, next_block):
        def cond(state):
            return jnp.logical_and(state[0] >= 0, state[1] > LOG_WEIGHT_FLOOR)

        def body(state):
            j = state[0]
            ks = pl.ds(pl.multiple_of(j * TK, TK), TK)
            runs, accs = walk(q_heads, [(ks, False)], state[2:4], state[4:6])
            return (j - 1, jnp.maximum(jnp.max(runs[0]), jnp.max(runs[1])), *runs, *accs)

        run_max = jnp.maximum(jnp.max(runs[0]), jnp.max(runs[1]))
        state = lax.while_loop(cond, body, (next_block, run_max, *runs, *accs))
        out = jnp.where(lane_head == 0, state[4], state[5])
        o_ref[qs, :] = out.astype(o_ref.dtype)

    zero_runs = [jnp.zeros((TQ, 1), F32)] * HEADS_PER_BLOCK
    zero_accs = [jnp.zeros((TQ, LANES), F32)] * HEADS_PER_BLOCK

    first = pl.ds(0, TQ)
    q_heads = split_heads(first)
    runs, accs = walk(q_heads, [(first, True)], zero_runs, zero_accs)
    finish(first, q_heads, runs, accs, jnp.int32(-1))

    def q_block(qi, carry):
        qs = pl.ds(pl.multiple_of(qi * TQ, TQ), TQ)
        previous = pl.ds(pl.multiple_of((qi - 1) * TK, TK), TK)
        q_heads = split_heads(qs)
        runs, accs = walk(q_heads, [(qs, True), (previous, False)], zero_runs, zero_accs)
        finish(qs, q_heads, runs, accs, qi - 2)
        return carry

    lax.fori_loop(1, seq // TQ, q_block, 0)


def _stick_breaking_attention(q, k, v):
    bsz, seq, d = q.shape
    spec = pl.BlockSpec((None, seq, LANES), lambda b, p: (b, 0, p))
    return pl.pallas_call(
        _attn_kernel,
        grid=(bsz, d // LANES),
        in_specs=[spec, spec, spec],
        out_specs=spec,
        out_shape=jax.ShapeDtypeStruct(q.shape, BF16),
        compiler_params=_params(2),
        name="stick_breaking_attention",
    )(q, k, v)


def _oproj_kernel(x_ref, o_ref, mod_ref, wout_ref, y_ref):
    y_ref[...] = x_ref[...] + mod_ref[2:3, :] * _dot(o_ref[...], wout_ref[...])


def _out_project(x, o, mod, layer, w_out):
    bsz, seq, d = x.shape
    tm = TM_DENSE
    return pl.pallas_call(
        _oproj_kernel,
        grid=(bsz, seq // tm),
        in_specs=[
            pl.BlockSpec((None, tm, d), lambda b, i: (b, i, 0)),
            pl.BlockSpec((None, tm, d), lambda b, i: (b, i, 0)),
            pl.BlockSpec((None, None, 6, d), lambda b, i: (layer, b, 0, 0)),
            _resident((d, d)),
        ],
        out_specs=pl.BlockSpec((None, tm, d), lambda b, i: (b, i, 0)),
        out_shape=jax.ShapeDtypeStruct(x.shape, F32),
        compiler_params=_params(2),
        name="attn_out_project",
    )(x, o, mod, w_out)


def kernel(x, c, mod_w, mod_b, norm_g, mlp_w_up, mlp_w_down, a_w_in, a_v_norm_g,
           a_w_spatial, a_b_spatial, a_w_out, kv_mod_w, kv_mod_b, kv_norm_g, kv_w,
           k_norm_g, b_w_q, q_norm_g, b_w_out):
    d = x.shape[-1]
    mod = _mod_project(c, mod_w, mod_b)
    kvmod = _mod_project(c, kv_mod_w[None], kv_mod_b[None])
    k = v = None
    for layer in range(DEPTH):
        token_gain = norm_g[layer, 0].reshape(1, d)
        channel_gain = norm_g[layer, 1].reshape(1, d)
        if layer < N_A_LAYERS:
            x = _gmlp_layer(
                x, mod, layer, token_gain, a_w_in[layer].astype(BF16),
                a_v_norm_g[layer].reshape(1, GMLP_WIDTH), a_w_spatial[layer],
                a_b_spatial[layer].T, a_w_out[layer].astype(BF16))
        else:
            j = layer - N_A_LAYERS
            q = _q_project(x, mod, layer, token_gain, b_w_q[j].astype(BF16),
                           jnp.tile(q_norm_g[j], N_HEADS).reshape(1, d))
            o = _stick_breaking_attention(q, k, v)
            x = _out_project(x, o, mod, layer, b_w_out[j].astype(BF16))
        x = _mlp_layer(x, mod, layer, channel_gain, mlp_w_up[layer].astype(BF16),
                       mlp_w_down[layer].astype(BF16))
        if layer == N_A_LAYERS - 1:
            k, v = _kv_project(x, kvmod, kv_norm_g.reshape(1, d), kv_w.astype(BF16),
                               jnp.tile(k_norm_g, N_HEADS).reshape(1, d))
    return x
```

```python
import functools

import jax
import jax.numpy as jnp
from jax import lax
from jax.experimental import pallas as pl
from jax.experimental.pallas import tpu as pltpu

D_MODEL = 1024
DEPTH = 4
N_A_LAYERS = DEPTH // 2
CHUNK = 128
GMLP_WIDTH = 2 * D_MODEL
GMLP_GROUPS = 8
GROUP_WIDTH = GMLP_WIDTH // GMLP_GROUPS
N_HEADS = 16
HEAD_DIM = D_MODEL // N_HEADS
D_FF = 4 * D_MODEL
EPS = 1e-6

LANES = 128
HEADS_PER_BLOCK = LANES // HEAD_DIM
ATTN_HEADS = 4
ATTN_WIDTH = ATTN_HEADS * HEAD_DIM
VMEM_LIMIT_BYTES = 56 * 1024 * 1024

TM_DENSE = 512
TQ = 256
TK = TQ
LOG_WEIGHT_FLOOR = -104.0

F32 = jnp.float32
BF16 = jnp.bfloat16


def _dot(a, b):
    return jnp.dot(a, b, preferred_element_type=F32)


def _dot_hi_lo(a_f32, b_bf16):
    hi = a_f32.astype(BF16)
    lo = (a_f32 - hi.astype(F32)).astype(BF16)
    return _dot(hi, b_bf16) + _dot(lo, b_bf16)


def _norm_modulate(x, gain, shift, scale):
    ms = jnp.mean(x * x, axis=-1, keepdims=True)
    h = x * lax.rsqrt(ms + EPS) * gain
    return h * (1.0 + scale) + shift


def _resident(shape):
    zeros = (0,) * len(shape)
    return pl.BlockSpec(shape, lambda *_: zeros, pipeline_mode=pl.Buffered(1))


def _params(n_grid_axes):
    return pltpu.CompilerParams(
        dimension_semantics=("arbitrary",) * n_grid_axes,
        vmem_limit_bytes=VMEM_LIMIT_BYTES)


def _mod_kernel(c_ref, w_ref, b_ref, o_ref):
    o_ref[...] = _dot(c_ref[...].astype(BF16), w_ref[...].astype(BF16)) + b_ref[...]


def _mod_project(c, w, b):
    n_layers, d, nd = w.shape
    n_vec = nd // d
    bsz = c.shape[0]
    out = pl.pallas_call(
        _mod_kernel,
        grid=(n_layers, n_vec),
        in_specs=[
            pl.BlockSpec((bsz, d), lambda l, n: (0, 0)),
            pl.BlockSpec((None, d, d), lambda l, n: (l, 0, n)),
            pl.BlockSpec((None, 1, d), lambda l, n: (l, 0, n)),
        ],
        out_specs=pl.BlockSpec((None, None, bsz, d), lambda l, n: (l, n, 0, 0)),
        out_shape=jax.ShapeDtypeStruct((n_layers, n_vec, bsz, d), F32),
        compiler_params=_params(2),
        name="adaln_project",
    )(c, w, b.reshape(n_layers, 1, nd))
    return out.transpose(0, 2, 1, 3)


def _gmlp_kernel(x_ref, mod_ref, ng_ref, win_ref, vg_ref, ws_ref, bs_ref,
                 wout_ref, o_ref, gated_ref, *, tm):
    x = x_ref[...]
    h = _norm_modulate(x, ng_ref[...], mod_ref[0:1, :], mod_ref[1:2, :])
    uv = _dot(h.astype(BF16), win_ref[...])
    uv = 0.5 * uv * (1.0 + lax.erf(uv * (0.5 ** 0.5)))
    u = uv[:, :GMLP_WIDTH]
    v = uv[:, GMLP_WIDTH:]
    vms = jnp.mean(v * v, axis=-1, keepdims=True)
    vb = (v * lax.rsqrt(vms + EPS) * vg_ref[...]).astype(BF16)

    row = lax.broadcasted_iota(jnp.int32, (CHUNK, CHUNK), 0)
    col = lax.broadcasted_iota(jnp.int32, (CHUNK, CHUNK), 1)
    tril = col <= row
    for g in range(GMLP_GROUPS):
        w_causal = jnp.where(tril, ws_ref[g], 0.0).astype(BF16)
        bias = bs_ref[:, g:g + 1]
        cs = slice(g * GROUP_WIDTH, (g + 1) * GROUP_WIDTH)
        for n in range(tm // CHUNK):
            rs = slice(n * CHUNK, (n + 1) * CHUNK)
            z = _dot(w_causal, vb[rs, cs]) + bias
            gated_ref[rs, cs] = (u[rs, cs] * z).astype(BF16)
    y = _dot(gated_ref[...], wout_ref[...])
    o_ref[...] = x + mod_ref[2:3, :] * y


def _gmlp_layer(x, mod, layer, norm_g, w_in, v_norm_g, w_spatial, b_spatial_t, w_out):
    bsz, seq, d = x.shape
    tm = TM_DENSE
    return pl.pallas_call(
        functools.partial(_gmlp_kernel, tm=tm),
        grid=(bsz, seq // tm),
        in_specs=[
            pl.BlockSpec((None, tm, d), lambda b, i: (b, i, 0)),
            pl.BlockSpec((None, None, 6, d), lambda b, i: (layer, b, 0, 0)),
            _resident((1, d)),
            _resident((d, 2 * GMLP_WIDTH)),
            _resident((1, GMLP_WIDTH)),
            _resident((GMLP_GROUPS, CHUNK, CHUNK)),
            _resident((CHUNK, GMLP_GROUPS)),
            _resident((GMLP_WIDTH, d)),
        ],
        out_specs=pl.BlockSpec((None, tm, d), lambda b, i: (b, i, 0)),
        out_shape=jax.ShapeDtypeStruct(x.shape, F32),
        scratch_shapes=[pltpu.VMEM((tm, GMLP_WIDTH), BF16)],
        compiler_params=_params(2),
        name="gmlp_mixer",
    )(x, mod, norm_g, w_in, v_norm_g, w_spatial, b_spatial_t, w_out)


def _mlp_kernel(x_ref, mod_ref, ng_ref, wup_ref, wdn_ref, o_ref):
    x = x_ref[...]
    h = _norm_modulate(x, ng_ref[...], mod_ref[3:4, :], mod_ref[4:5, :])
    a = jnp.maximum(_dot(h.astype(BF16), wup_ref[...]), 0.0)
    y = _dot((a * a).astype(BF16), wdn_ref[...])
    o_ref[...] = x + mod_ref[5:6, :] * y


def _mlp_layer(x, mod, layer, norm_g, w_up, w_down):
    bsz, seq, d = x.shape
    tm = TM_DENSE
    return pl.pallas_call(
        _mlp_kernel,
        grid=(bsz, seq // tm),
        in_specs=[
            pl.BlockSpec((None, tm, d), lambda b, i: (b, i, 0)),
            pl.BlockSpec((None, None, 6, d), lambda b, i: (layer, b, 0, 0)),
            _resident((1, d)),
            _resident((d, D_FF)),
            _resident((D_FF, d)),
        ],
        out_specs=pl.BlockSpec((None, tm, d), lambda b, i: (b, i, 0)),
        out_shape=jax.ShapeDtypeStruct(x.shape, F32),
        compiler_params=_params(2),
        name="relu2_mlp",
    )(x, mod, norm_g, w_up, w_down)


def _head_rmsnorm(t, gain_tiled, out_scale):
    row = lax.broadcasted_iota(jnp.int32, (LANES, LANES), 0) // HEAD_DIM
    col = lax.broadcasted_iota(jnp.int32, (LANES, LANES), 1) // HEAD_DIM
    same_head = jnp.where(row == col, 1.0, 0.0).astype(BF16)
    pieces = []
    for j in range(t.shape[1] // LANES):
        tj = t[:, j * LANES:(j + 1) * LANES]
        ms = _dot_hi_lo(tj * tj, same_head) * (1.0 / HEAD_DIM)
        gj = gain_tiled[:, j * LANES:(j + 1) * LANES]
        pieces.append(tj * lax.rsqrt(ms + EPS) * (gj * out_scale))
    return pieces


def _qproj_kernel(x_ref, mod_ref, ng_ref, wq_ref, qg_ref, q_ref):
    h = _norm_modulate(x_ref[...], ng_ref[...], mod_ref[0:1, :], mod_ref[1:2, :])
    q = _dot(h.astype(BF16), wq_ref[...])
    for j, piece in enumerate(_head_rmsnorm(q, qg_ref[...], HEAD_DIM ** -0.5)):
        q_ref[:, j * LANES:(j + 1) * LANES] = piece.astype(BF16)


def _q_project(x, mod, layer, norm_g, w_q, q_gain_tiled):
    bsz, seq, d = x.shape
    tm = TM_DENSE
    return pl.pallas_call(
        _qproj_kernel,
        grid=(bsz, seq // tm),
        in_specs=[
            pl.BlockSpec((None, tm, d), lambda b, i: (b, i, 0)),
            pl.BlockSpec((None, None, 6, d), lambda b, i: (layer, b, 0, 0)),
            _resident((1, d)),
            _resident((d, d)),
            _resident((1, d)),
        ],
        out_specs=pl.BlockSpec((None, tm, d), lambda b, i: (b, i, 0)),
        out_shape=jax.ShapeDtypeStruct(x.shape, BF16),
        compiler_params=_params(2),
        name="q_project",
    )(x, mod, norm_g, w_q, q_gain_tiled)


def _kvproj_kernel(x_ref, mod_ref, ng_ref, wkv_ref, kg_ref, k_ref, v_ref):
    h = _norm_modulate(x_ref[...], ng_ref[...], mod_ref[0:1, :], mod_ref[1:2, :])
    kv = _dot(h.astype(BF16), wkv_ref[...])
    for j, piece in enumerate(_head_rmsnorm(kv[:, :D_MODEL], kg_ref[...], 1.0)):
        k_ref[:, j * LANES:(j + 1) * LANES] = piece.astype(BF16)
    v_ref[...] = kv[:, D_MODEL:].astype(BF16)


def _kv_project(x, kvmod, norm_g, w_kv, k_gain_tiled):
    bsz, seq, d = x.shape
    tm = TM_DENSE
    return pl.pallas_call(
        _kvproj_kernel,
        grid=(bsz, seq // tm),
        in_specs=[
            pl.BlockSpec((None, tm, d), lambda b, i: (b, i, 0)),
            pl.BlockSpec((None, None, 2, d), lambda b, i: (0, b, 0, 0)),
            _resident((1, d)),
            _resident((d, 2 * d)),
            _resident((1, d)),
        ],
        out_specs=[pl.BlockSpec((None, tm, d), lambda b, i: (b, i, 0))] * 2,
        out_shape=[jax.ShapeDtypeStruct(x.shape, BF16)] * 2,
        compiler_params=_params(2),
        name="kv_project",
    )(x, kvmod, norm_g, w_kv, k_gain_tiled)


def _attn_kernel(q_ref, k_ref, v_ref, o_ref):
    seq = q_ref.shape[0]
    lane_head = lax.broadcasted_iota(jnp.int32, (TQ, LANES), 1) // HEAD_DIM
    row = lax.broadcasted_iota(jnp.int32, (TQ, TK), 0)
    col = lax.broadcasted_iota(jnp.int32, (TQ, TK), 1)
    strictly_causal = col < row
    suffix = jnp.where(row >= col, 1.0, 0.0).astype(BF16)

    def pair_lanes(head):
        pair = head // HEADS_PER_BLOCK
        return slice(pair * LANES, (pair + 1) * LANES)

    def logits(qh, kb):
        return lax.dot_general(qh, kb, (((1,), (1,)), ((), ())),
                               preferred_element_type=F32)

    def keep_sums(z, diagonal):
        neg_abs = pltpu.bitcast(pltpu.bitcast(z, jnp.uint32) | jnp.uint32(0x80000000), F32)
        keep = jnp.maximum(z, 0.0) + jnp.log(1.0 + jnp.exp(neg_abs))
        if diagonal:
            keep = jnp.where(strictly_causal, keep, 0.0)
        return _dot_hi_lo(keep, suffix), jnp.sum(keep, axis=-1, keepdims=True)

    def weights(z, incl, run, diagonal):
        w = jnp.exp(z - incl - run)
        if diagonal:
            w = jnp.where(strictly_causal, w, 0.0)
        return w.astype(BF16)

    def walk(q_heads, blocks, runs, accs):
        heads = range(ATTN_HEADS)
        zs = [[logits(q_heads[h], k_ref[ks, pair_lanes(h)]) for h in heads] for ks, _ in blocks]
        sums = [[keep_sums(zs[b][h], diagonal) for h in heads]
                for b, (_, diagonal) in enumerate(blocks)]
        runs, accs = list(runs), list(accs)
        for b, (ks, diagonal) in enumerate(blocks):
            for h in heads:
                incl, row_sum = sums[b][h]
                w = weights(zs[b][h], incl, runs[h], diagonal)
                accs[h] = accs[h] + _dot(w, v_ref[ks, pair_lanes(h)])
                runs[h] = runs[h] + row_sum
        return runs, accs

    def split_heads(qs):
        out = []
        for h in range(ATTN_HEADS):
            q = q_ref[qs, pair_lanes(h)]
            out.append(jnp.where(lane_head == h % HEADS_PER_BLOCK, q, jnp.zeros_like(q)))
        return out

    def min_run(runs):
        return functools.reduce(jnp.minimum, [jnp.min(r) for r in runs])

    def finish(qs, q_heads, runs, accs, next_block):
        def cond(state):
            return jnp.logical_and(state[0] >= 0, state[1] < -LOG_WEIGHT_FLOOR)

        def body(state):
            j = state[0]
            ks = pl.ds(pl.multiple_of(j * TK, TK), TK)
            runs, accs = walk(q_heads, [(ks, False)], state[2:2 + ATTN_HEADS],
                              state[2 + ATTN_HEADS:])
            return (j - 1, min_run(runs), *runs, *accs)

        state = lax.while_loop(cond, body, (next_block, min_run(runs), *runs, *accs))
        accs = state[2 + ATTN_HEADS:]
        for pair in range(ATTN_HEADS // HEADS_PER_BLOCK):
            out = jnp.where(lane_head == 0, accs[2 * pair], accs[2 * pair + 1])
            o_ref[qs, pair * LANES:(pair + 1) * LANES] = out.astype(o_ref.dtype)

    zero_runs = [jnp.zeros((TQ, 1), F32)] * ATTN_HEADS
    zero_accs = [jnp.zeros((TQ, LANES), F32)] * ATTN_HEADS

    first = pl.ds(0, TQ)
    q_heads = split_heads(first)
    runs, accs = walk(q_heads, [(first, True)], zero_runs, zero_accs)
    finish(first, q_heads, runs, accs, jnp.int32(-1))

    def q_block(qi, carry):
        qs = pl.ds(pl.multiple_of(qi * TQ, TQ), TQ)
        previous = pl.ds(pl.multiple_of((qi - 1) * TK, TK), TK)
        q_heads = split_heads(qs)
        runs, accs = walk(q_heads, [(qs, True), (previous, False)], zero_runs, zero_accs)
        finish(qs, q_heads, runs, accs, qi - 2)
        return carry

    lax.fori_loop(1, seq // TQ, q_block, 0)


def _stick_breaking_attention(q, k, v):
    bsz, seq, d = q.shape
    spec = pl.BlockSpec((None, seq, ATTN_WIDTH), lambda b, p: (b, 0, p))
    return pl.pallas_call(
        _attn_kernel,
        grid=(bsz, d // ATTN_WIDTH),
        in_specs=[spec, spec, spec],
        out_specs=spec,
        out_shape=jax.ShapeDtypeStruct(q.shape, BF16),
        compiler_params=_params(2),
        name="stick_breaking_attention",
    )(q, k, v)


def _oproj_kernel(x_ref, o_ref, mod_ref, wout_ref, y_ref):
    y_ref[...] = x_ref[...] + mod_ref[2:3, :] * _dot(o_ref[...], wout_ref[...])


def _out_project(x, o, mod, layer, w_out):
    bsz, seq, d = x.shape
    tm = TM_DENSE
    return pl.pallas_call(
        _oproj_kernel,
        grid=(bsz, seq // tm),
        in_specs=[
            pl.BlockSpec((None, tm, d), lambda b, i: (b, i, 0)),
            pl.BlockSpec((None, tm, d), lambda b, i: (b, i, 0)),
            pl.BlockSpec((None, None, 6, d), lambda b, i: (layer, b, 0, 0)),
            _resident((d, d)),
        ],
        out_specs=pl.BlockSpec((None, tm, d), lambda b, i: (b, i, 0)),
        out_shape=jax.ShapeDtypeStruct(x.shape, F32),
        compiler_params=_params(2),
        name="attn_out_project",
    )(x, o, mod, w_out)


def kernel(x, c, mod_w, mod_b, norm_g, mlp_w_up, mlp_w_down, a_w_in, a_v_norm_g,
           a_w_spatial, a_b_spatial, a_w_out, kv_mod_w, kv_mod_b, kv_norm_g, kv_w,
           k_norm_g, b_w_q, q_norm_g, b_w_out):
    d = x.shape[-1]
    mod = _mod_project(c, mod_w, mod_b)
    kvmod = _mod_project(c, kv_mod_w[None], kv_mod_b[None])
    k = v = None
    for layer in range(DEPTH):
        token_gain = norm_g[layer, 0].reshape(1, d)
        channel_gain = norm_g[layer, 1].reshape(1, d)
        if layer < N_A_LAYERS:
            x = _gmlp_layer(
                x, mod, layer, token_gain, a_w_in[layer].astype(BF16),
                a_v_norm_g[layer].reshape(1, GMLP_WIDTH), a_w_spatial[layer],
                a_b_spatial[layer].T, a_w_out[layer].astype(BF16))
        else:
            j = layer - N_A_LAYERS
            q = _q_project(x, mod, layer, token_gain, b_w_q[j].astype(BF16),
                           jnp.tile(q_norm_g[j], N_HEADS).reshape(1, d))
            o = _stick_breaking_attention(q, k, v)
            x = _out_project(x, o, mod, layer, b_w_out[j].astype(BF16))
        x = _mlp_layer(x, mod, layer, channel_gain, mlp_w_up[layer].astype(BF16),
                       mlp_w_down[layer].astype(BF16))
        if layer == N_A_LAYERS - 1:
            k, v = _kv_project(x, kvmod, kv_norm_g.reshape(1, d), kv_w.astype(BF16),
                               jnp.tile(k_norm_g, N_HEADS).reshape(1, d))
    return x
```

```python
import functools

import jax
import jax.numpy as jnp
from jax import lax
from jax.experimental import pallas as pl
from jax.experimental.pallas import tpu as pltpu

D_MODEL = 1024
DEPTH = 4
N_A_LAYERS = DEPTH // 2
CHUNK = 128
GMLP_WIDTH = 2 * D_MODEL
GMLP_GROUPS = 8
GROUP_WIDTH = GMLP_WIDTH // GMLP_GROUPS
N_HEADS = 16
HEAD_DIM = D_MODEL // N_HEADS
D_FF = 4 * D_MODEL
EPS = 1e-6

LANES = 128
HEADS_PER_BLOCK = LANES // HEAD_DIM
ATTN_HEADS = 4
ATTN_WIDTH = ATTN_HEADS * HEAD_DIM
VMEM_LIMIT_BYTES = 56 * 1024 * 1024

TM_DENSE = 512
TQ = 256
TK = TQ
LOG_WEIGHT_FLOOR = -104.0
NEG_LOG2_E = -1.4426950408889634

F32 = jnp.float32
BF16 = jnp.bfloat16


def _dot(a, b):
    return jnp.dot(a, b, preferred_element_type=F32)


def _dot_hi_lo(a_f32, b_bf16):
    hi = a_f32.astype(BF16)
    lo = (a_f32 - hi.astype(F32)).astype(BF16)
    return _dot(hi, b_bf16) + _dot(lo, b_bf16)


def _norm_modulate(x, gain, shift, scale):
    ms = jnp.mean(x * x, axis=-1, keepdims=True)
    h = x * lax.rsqrt(ms + EPS) * gain
    return h * (1.0 + scale) + shift


def _resident(shape):
    zeros = (0,) * len(shape)
    return pl.BlockSpec(shape, lambda *_: zeros, pipeline_mode=pl.Buffered(1))


def _params(n_grid_axes):
    return pltpu.CompilerParams(
        dimension_semantics=("arbitrary",) * n_grid_axes,
        vmem_limit_bytes=VMEM_LIMIT_BYTES)


def _mod_kernel(c_ref, w_ref, b_ref, o_ref):
    o_ref[...] = _dot(c_ref[...].astype(BF16), w_ref[...].astype(BF16)) + b_ref[...]


def _mod_project(c, w, b):
    n_layers, d, nd = w.shape
    n_vec = nd // d
    bsz = c.shape[0]
    out = pl.pallas_call(
        _mod_kernel,
        grid=(n_layers, n_vec),
        in_specs=[
            pl.BlockSpec((bsz, d), lambda l, n: (0, 0)),
            pl.BlockSpec((None, d, d), lambda l, n: (l, 0, n)),
            pl.BlockSpec((None, 1, d), lambda l, n: (l, 0, n)),
        ],
        out_specs=pl.BlockSpec((None, None, bsz, d), lambda l, n: (l, n, 0, 0)),
        out_shape=jax.ShapeDtypeStruct((n_layers, n_vec, bsz, d), F32),
        compiler_params=_params(2),
        name="adaln_project",
    )(c, w, b.reshape(n_layers, 1, nd))
    return out.transpose(0, 2, 1, 3)


def _gmlp_kernel(x_ref, mod_ref, ng_ref, win_ref, vg_ref, ws_ref, bs_ref,
                 wout_ref, o_ref, gated_ref, *, tm):
    x = x_ref[...]
    h = _norm_modulate(x, ng_ref[...], mod_ref[0:1, :], mod_ref[1:2, :])
    uv = _dot(h.astype(BF16), win_ref[...])
    uv = 0.5 * uv * (1.0 + lax.erf(uv * (0.5 ** 0.5)))
    u = uv[:, :GMLP_WIDTH]
    v = uv[:, GMLP_WIDTH:]
    vms = jnp.mean(v * v, axis=-1, keepdims=True)
    vb = (v * lax.rsqrt(vms + EPS) * vg_ref[...]).astype(BF16)

    row = lax.broadcasted_iota(jnp.int32, (CHUNK, CHUNK), 0)
    col = lax.broadcasted_iota(jnp.int32, (CHUNK, CHUNK), 1)
    tril = col <= row
    for g in range(GMLP_GROUPS):
        w_causal = jnp.where(tril, ws_ref[g], 0.0).astype(BF16)
        bias = bs_ref[:, g:g + 1]
        cs = slice(g * GROUP_WIDTH, (g + 1) * GROUP_WIDTH)
        for n in range(tm // CHUNK):
            rs = slice(n * CHUNK, (n + 1) * CHUNK)
            z = _dot(w_causal, vb[rs, cs]) + bias
            gated_ref[rs, cs] = (u[rs, cs] * z).astype(BF16)
    y = _dot(gated_ref[...], wout_ref[...])
    o_ref[...] = x + mod_ref[2:3, :] * y


def _gmlp_layer(x, mod, layer, norm_g, w_in, v_norm_g, w_spatial, b_spatial_t, w_out):
    bsz, seq, d = x.shape
    tm = TM_DENSE
    return pl.pallas_call(
        functools.partial(_gmlp_kernel, tm=tm),
        grid=(bsz, seq // tm),
        in_specs=[
            pl.BlockSpec((None, tm, d), lambda b, i: (b, i, 0)),
            pl.BlockSpec((None, None, 6, d), lambda b, i: (layer, b, 0, 0)),
            _resident((1, d)),
            _resident((d, 2 * GMLP_WIDTH)),
            _resident((1, GMLP_WIDTH)),
            _resident((GMLP_GROUPS, CHUNK, CHUNK)),
            _resident((CHUNK, GMLP_GROUPS)),
            _resident((GMLP_WIDTH, d)),
        ],
        out_specs=pl.BlockSpec((None, tm, d), lambda b, i: (b, i, 0)),
        out_shape=jax.ShapeDtypeStruct(x.shape, F32),
        scratch_shapes=[pltpu.VMEM((tm, GMLP_WIDTH), BF16)],
        compiler_params=_params(2),
        name="gmlp_mixer",
    )(x, mod, norm_g, w_in, v_norm_g, w_spatial, b_spatial_t, w_out)


def _mlp_residual(x, mod_ref, ng_ref, wup_ref, wdn_ref):
    h = _norm_modulate(x, ng_ref[...], mod_ref[3:4, :], mod_ref[4:5, :])
    a = jnp.maximum(_dot(h.astype(BF16), wup_ref[...]), 0.0)
    y = _dot((a * a).astype(BF16), wdn_ref[...])
    return x + mod_ref[5:6, :] * y


def _mlp_kernel(x_ref, mod_ref, ng_ref, wup_ref, wdn_ref, o_ref):
    o_ref[...] = _mlp_residual(x_ref[...], mod_ref, ng_ref, wup_ref, wdn_ref)


def _attn_out_mlp_kernel(x_ref, a_ref, wout_ref, mod_ref, ng_ref, wup_ref, wdn_ref, o_ref):
    x = x_ref[...] + mod_ref[2:3, :] * _dot(a_ref[...], wout_ref[...])
    o_ref[...] = _mlp_residual(x, mod_ref, ng_ref, wup_ref, wdn_ref)


def _mlp_layer(x, mod, layer, norm_g, w_up, w_down, attn=None, w_attn_out=None):
    bsz, seq, d = x.shape
    tm = TM_DENSE
    tile = pl.BlockSpec((None, tm, d), lambda b, i: (b, i, 0))
    operands, specs, body = [x], [tile], _mlp_kernel
    if attn is not None:
        operands += [attn, w_attn_out]
        specs += [tile, _resident((d, d))]
        body = _attn_out_mlp_kernel
    return pl.pallas_call(
        body,
        grid=(bsz, seq // tm),
        in_specs=specs + [
            pl.BlockSpec((None, None, 6, d), lambda b, i: (layer, b, 0, 0)),
            _resident((1, d)),
            _resident((d, D_FF)),
            _resident((D_FF, d)),
        ],
        out_specs=tile,
        out_shape=jax.ShapeDtypeStruct(x.shape, F32),
        compiler_params=_params(2),
        name="relu2_mlp",
    )(*operands, mod, norm_g, w_up, w_down)


def _head_rmsnorm(t, gain_tiled, out_scale):
    row = lax.broadcasted_iota(jnp.int32, (LANES, LANES), 0) // HEAD_DIM
    col = lax.broadcasted_iota(jnp.int32, (LANES, LANES), 1) // HEAD_DIM
    same_head = jnp.where(row == col, 1.0, 0.0).astype(BF16)
    pieces = []
    for j in range(t.shape[1] // LANES):
        tj = t[:, j * LANES:(j + 1) * LANES]
        ms = _dot_hi_lo(tj * tj, same_head) * (1.0 / HEAD_DIM)
        gj = gain_tiled[:, j * LANES:(j + 1) * LANES]
        pieces.append(tj * lax.rsqrt(ms + EPS) * (gj * out_scale))
    return pieces


def _qproj_kernel(x_ref, mod_ref, ng_ref, wq_ref, qg_ref, q_ref):
    h = _norm_modulate(x_ref[...], ng_ref[...], mod_ref[0:1, :], mod_ref[1:2, :])
    q = _dot(h.astype(BF16), wq_ref[...])
    for j, piece in enumerate(_head_rmsnorm(q, qg_ref[...], HEAD_DIM ** -0.5)):
        q_ref[:, j * LANES:(j + 1) * LANES] = piece.astype(BF16)


def _q_project(x, mod, layer, norm_g, w_q, q_gain_tiled):
    bsz, seq, d = x.shape
    tm = TM_DENSE
    return pl.pallas_call(
        _qproj_kernel,
        grid=(bsz, seq // tm),
        in_specs=[
            pl.BlockSpec((None, tm, d), lambda b, i: (b, i, 0)),
            pl.BlockSpec((None, None, 6, d), lambda b, i: (layer, b, 0, 0)),
            _resident((1, d)),
            _resident((d, d)),
            _resident((1, d)),
        ],
        out_specs=pl.BlockSpec((None, tm, d), lambda b, i: (b, i, 0)),
        out_shape=jax.ShapeDtypeStruct(x.shape, BF16),
        compiler_params=_params(2),
        name="q_project",
    )(x, mod, norm_g, w_q, q_gain_tiled)


def _kvproj_kernel(x_ref, mod_ref, ng_ref, wkv_ref, kg_ref, k_ref, v_ref):
    h = _norm_modulate(x_ref[...], ng_ref[...], mod_ref[0:1, :], mod_ref[1:2, :])
    kv = _dot(h.astype(BF16), wkv_ref[...])
    for j, piece in enumerate(_head_rmsnorm(kv[:, :D_MODEL], kg_ref[...], 1.0)):
        k_ref[:, j * LANES:(j + 1) * LANES] = piece.astype(BF16)
    v_ref[...] = kv[:, D_MODEL:].astype(BF16)


def _kv_project(x, kvmod, norm_g, w_kv, k_gain_tiled):
    bsz, seq, d = x.shape
    tm = TM_DENSE
    return pl.pallas_call(
        _kvproj_kernel,
        grid=(bsz, seq // tm),
        in_specs=[
            pl.BlockSpec((None, tm, d), lambda b, i: (b, i, 0)),
            pl.BlockSpec((None, None, 2, d), lambda b, i: (0, b, 0, 0)),
            _resident((1, d)),
            _resident((d, 2 * d)),
            _resident((1, d)),
        ],
        out_specs=[pl.BlockSpec((None, tm, d), lambda b, i: (b, i, 0))] * 2,
        out_shape=[jax.ShapeDtypeStruct(x.shape, BF16)] * 2,
        compiler_params=_params(2),
        name="kv_project",
    )(x, kvmod, norm_g, w_kv, k_gain_tiled)


def _attn_kernel(q_ref, k_ref, v_ref, o_ref):
    seq = q_ref.shape[0]
    lane_head = lax.broadcasted_iota(jnp.int32, (TQ, LANES), 1) // HEAD_DIM
    row = lax.broadcasted_iota(jnp.int32, (TQ, TK), 0)
    col = lax.broadcasted_iota(jnp.int32, (TQ, TK), 1)
    strictly_causal = col < row
    suffix = jnp.where(row >= col, 1.0, 0.0).astype(BF16)

    def pair_lanes(head):
        pair = head // HEADS_PER_BLOCK
        return slice(pair * LANES, (pair + 1) * LANES)

    def logits(qh, kb):
        return lax.dot_general(qh, kb, (((1,), (1,)), ((), ())),
                               preferred_element_type=F32)

    def keep_sums(z, diagonal):
        keep = jnp.maximum(z, 0.0) + jnp.log(1.0 + jnp.exp2(jnp.abs(z) * NEG_LOG2_E))
        if diagonal:
            keep = jnp.where(strictly_causal, keep, 0.0)
        return _dot_hi_lo(keep, suffix), jnp.sum(keep, axis=-1, keepdims=True)

    def weights(z, incl, run, diagonal):
        w = jnp.exp(z - incl - run)
        if diagonal:
            w = jnp.where(strictly_causal, w, 0.0)
        return w.astype(BF16)

    def walk(q_heads, blocks, runs, accs):
        heads = range(ATTN_HEADS)
        zs = [[logits(q_heads[h], k_ref[ks, pair_lanes(h)]) for h in heads] for ks, _ in blocks]
        sums = [[keep_sums(zs[b][h], diagonal) for h in heads]
                for b, (_, diagonal) in enumerate(blocks)]
        runs, accs = list(runs), list(accs)
        for b, (ks, diagonal) in enumerate(blocks):
            for h in heads:
                incl, row_sum = sums[b][h]
                w = weights(zs[b][h], incl, runs[h], diagonal)
                accs[h] = accs[h] + _dot(w, v_ref[ks, pair_lanes(h)])
                runs[h] = runs[h] + row_sum
        return runs, accs

    def split_heads(qs):
        out = []
        for h in range(ATTN_HEADS):
            q = q_ref[qs, pair_lanes(h)]
            out.append(jnp.where(lane_head == h % HEADS_PER_BLOCK, q, jnp.zeros_like(q)))
        return out

    def min_run(runs):
        return functools.reduce(jnp.minimum, [jnp.min(r) for r in runs])

    def finish(qs, q_heads, runs, accs, next_block):
        def cond(state):
            return jnp.logical_and(state[0] >= 0, state[1] < -LOG_WEIGHT_FLOOR)

        def body(state):
            j = state[0]
            ks = pl.ds(pl.multiple_of(j * TK, TK), TK)
            runs, accs = walk(q_heads, [(ks, False)], state[2:2 + ATTN_HEADS],
                              state[2 + ATTN_HEADS:])
            return (j - 1, min_run(runs), *runs, *accs)

        state = lax.while_loop(cond, body, (next_block, min_run(runs), *runs, *accs))
        accs = state[2 + ATTN_HEADS:]
        for pair in range(ATTN_HEADS // HEADS_PER_BLOCK):
            out = jnp.where(lane_head == 0, accs[2 * pair], accs[2 * pair + 1])
            o_ref[qs, pair * LANES:(pair + 1) * LANES] = out.astype(o_ref.dtype)

    zero_runs = [jnp.zeros((TQ, 1), F32)] * ATTN_HEADS
    zero_accs = [jnp.zeros((TQ, LANES), F32)] * ATTN_HEADS

    first = pl.ds(0, TQ)
    q_heads = split_heads(first)
    runs, accs = walk(q_heads, [(first, True)], zero_runs, zero_accs)
    finish(first, q_heads, runs, accs, jnp.int32(-1))

    def q_block(qi, carry):
        qs = pl.ds(pl.multiple_of(qi * TQ, TQ), TQ)
        previous = pl.ds(pl.multiple_of((qi - 1) * TK, TK), TK)
        q_heads = split_heads(qs)
        runs, accs = walk(q_heads, [(qs, True), (previous, False)], zero_runs, zero_accs)
        finish(qs, q_heads, runs, accs, qi - 2)
        return carry

    lax.fori_loop(1, seq // TQ, q_block, 0)


def _stick_breaking_attention(q, k, v):
    bsz, seq, d = q.shape
    spec = pl.BlockSpec((None, seq, ATTN_WIDTH), lambda b, p: (b, 0, p))
    return pl.pallas_call(
        _attn_kernel,
        grid=(bsz, d // ATTN_WIDTH),
        in_specs=[spec, spec, spec],
        out_specs=spec,
        out_shape=jax.ShapeDtypeStruct(q.shape, BF16),
        compiler_params=_params(2),
        name="stick_breaking_attention",
    )(q, k, v)


def kernel(x, c, mod_w, mod_b, norm_g, mlp_w_up, mlp_w_down, a_w_in, a_v_norm_g,
           a_w_spatial, a_b_spatial, a_w_out, kv_mod_w, kv_mod_b, kv_norm_g, kv_w,
           k_norm_g, b_w_q, q_norm_g, b_w_out):
    d = x.shape[-1]
    mod = _mod_project(c, mod_w, mod_b)
    kvmod = _mod_project(c, kv_mod_w[None], kv_mod_b[None])
    k = v = None
    for layer in range(DEPTH):
        token_gain = norm_g[layer, 0].reshape(1, d)
        channel_gain = norm_g[layer, 1].reshape(1, d)
        attn = w_attn_out = None
        if layer < N_A_LAYERS:
            x = _gmlp_layer(
                x, mod, layer, token_gain, a_w_in[layer].astype(BF16),
                a_v_norm_g[layer].reshape(1, GMLP_WIDTH), a_w_spatial[layer],
                a_b_spatial[layer].T, a_w_out[layer].astype(BF16))
        else:
            j = layer - N_A_LAYERS
            q = _q_project(x, mod, layer, token_gain, b_w_q[j].astype(BF16),
                           jnp.tile(q_norm_g[j], N_HEADS).reshape(1, d))
            attn = _stick_breaking_attention(q, k, v)
            w_attn_out = b_w_out[j].astype(BF16)
        x = _mlp_layer(x, mod, layer, channel_gain, mlp_w_up[layer].astype(BF16),
                       mlp_w_down[layer].astype(BF16), attn, w_attn_out)
        if layer == N_A_LAYERS - 1:
            k, v = _kv_project(x, kvmod, kv_norm_g.reshape(1, d), kv_w.astype(BF16),
                               jnp.tile(k_norm_g, N_HEADS).reshape(1, d))
    return x
```

```python
import functools

import jax
import jax.numpy as jnp
from jax import lax
from jax.experimental import pallas as pl
from jax.experimental.pallas import tpu as pltpu

D_MODEL = 1024
DEPTH = 4
N_A_LAYERS = DEPTH // 2
CHUNK = 128
GMLP_WIDTH = 2 * D_MODEL
GMLP_GROUPS = 8
GROUP_WIDTH = GMLP_WIDTH // GMLP_GROUPS
N_HEADS = 16
HEAD_DIM = D_MODEL // N_HEADS
D_FF = 4 * D_MODEL
EPS = 1e-6

LANES = 128
HEADS_PER_BLOCK = LANES // HEAD_DIM
ATTN_HEADS = 4
ATTN_WIDTH = ATTN_HEADS * HEAD_DIM
VMEM_LIMIT_BYTES = 56 * 1024 * 1024

TM_DENSE = 512
TQ = 256
TK = TQ
LOG_WEIGHT_FLOOR = -104.0
NEG_LOG2_E = -1.4426950408889634

F32 = jnp.float32
BF16 = jnp.bfloat16


def _dot(a, b):
    return jnp.dot(a, b, preferred_element_type=F32)


def _dot_hi_lo(a_f32, b_bf16):
    hi = a_f32.astype(BF16)
    lo = (a_f32 - hi.astype(F32)).astype(BF16)
    return _dot(hi, b_bf16) + _dot(lo, b_bf16)


def _norm_modulate(x, gain, shift, scale):
    ms = jnp.mean(x * x, axis=-1, keepdims=True)
    h = x * lax.rsqrt(ms + EPS) * gain
    return h * (1.0 + scale) + shift


def _resident(shape):
    zeros = (0,) * len(shape)
    return pl.BlockSpec(shape, lambda *_: zeros, pipeline_mode=pl.Buffered(1))


def _params(n_grid_axes):
    return pltpu.CompilerParams(
        dimension_semantics=("arbitrary",) * n_grid_axes,
        vmem_limit_bytes=VMEM_LIMIT_BYTES)


def _mod_kernel(c_ref, w_ref, b_ref, o_ref):
    o_ref[...] = _dot(c_ref[...].astype(BF16), w_ref[...].astype(BF16)) + b_ref[...]


def _mod_project(c, w, b):
    n_layers, d, nd = w.shape
    n_vec = nd // d
    bsz = c.shape[0]
    out = pl.pallas_call(
        _mod_kernel,
        grid=(n_layers, n_vec),
        in_specs=[
            pl.BlockSpec((bsz, d), lambda l, n: (0, 0)),
            pl.BlockSpec((None, d, d), lambda l, n: (l, 0, n)),
            pl.BlockSpec((None, 1, d), lambda l, n: (l, 0, n)),
        ],
        out_specs=pl.BlockSpec((None, None, bsz, d), lambda l, n: (l, n, 0, 0)),
        out_shape=jax.ShapeDtypeStruct((n_layers, n_vec, bsz, d), F32),
        compiler_params=_params(2),
        name="adaln_project",
    )(c, w, b.reshape(n_layers, 1, nd))
    return out.transpose(0, 2, 1, 3)


def _gmlp_kernel(x_ref, mod_ref, ng_ref, win_ref, vg_ref, ws_ref, bs_ref,
                 wout_ref, o_ref, gated_ref, *, tm):
    x = x_ref[...]
    h = _norm_modulate(x, ng_ref[...], mod_ref[0:1, :], mod_ref[1:2, :])
    uv = _dot(h.astype(BF16), win_ref[...])
    uv = 0.5 * uv * (1.0 + lax.erf(uv * (0.5 ** 0.5)))
    u = uv[:, :GMLP_WIDTH]
    v = uv[:, GMLP_WIDTH:]
    vms = jnp.mean(v * v, axis=-1, keepdims=True)
    vb = (v * lax.rsqrt(vms + EPS) * vg_ref[...]).astype(BF16)

    row = lax.broadcasted_iota(jnp.int32, (CHUNK, CHUNK), 0)
    col = lax.broadcasted_iota(jnp.int32, (CHUNK, CHUNK), 1)
    tril = col <= row
    for g in range(GMLP_GROUPS):
        w_causal = jnp.where(tril, ws_ref[g], 0.0).astype(BF16)
        bias = bs_ref[:, g:g + 1]
        cs = slice(g * GROUP_WIDTH, (g + 1) * GROUP_WIDTH)
        for n in range(tm // CHUNK):
            rs = slice(n * CHUNK, (n + 1) * CHUNK)
            z = _dot(w_causal, vb[rs, cs]) + bias
            gated_ref[rs, cs] = (u[rs, cs] * z).astype(BF16)
    y = _dot(gated_ref[...], wout_ref[...])
    o_ref[...] = x + mod_ref[2:3, :] * y


def _gmlp_layer(x, mod, layer, norm_g, w_in, v_norm_g, w_spatial, b_spatial_t, w_out):
    bsz, seq, d = x.shape
    tm = TM_DENSE
    return pl.pallas_call(
        functools.partial(_gmlp_kernel, tm=tm),
        grid=(bsz, seq // tm),
        in_specs=[
            pl.BlockSpec((None, tm, d), lambda b, i: (b, i, 0)),
            pl.BlockSpec((None, None, 6, d), lambda b, i: (layer, b, 0, 0)),
            _resident((1, d)),
            _resident((d, 2 * GMLP_WIDTH)),
            _resident((1, GMLP_WIDTH)),
            _resident((GMLP_GROUPS, CHUNK, CHUNK)),
            _resident((CHUNK, GMLP_GROUPS)),
            _resident((GMLP_WIDTH, d)),
        ],
        out_specs=pl.BlockSpec((None, tm, d), lambda b, i: (b, i, 0)),
        out_shape=jax.ShapeDtypeStruct(x.shape, F32),
        scratch_shapes=[pltpu.VMEM((tm, GMLP_WIDTH), BF16)],
        compiler_params=_params(2),
        name="gmlp_mixer",
    )(x, mod, norm_g, w_in, v_norm_g, w_spatial, b_spatial_t, w_out)


def _mlp_residual(x, mod_ref, ng_ref, wup_ref, wdn_ref):
    h = _norm_modulate(x, ng_ref[...], mod_ref[3:4, :], mod_ref[4:5, :])
    a = jnp.maximum(_dot(h.astype(BF16), wup_ref[...]), 0.0)
    y = _dot((a * a).astype(BF16), wdn_ref[...])
    return x + mod_ref[5:6, :] * y


def _mlp_kernel(x_ref, mod_ref, ng_ref, wup_ref, wdn_ref, o_ref):
    o_ref[...] = _mlp_residual(x_ref[...], mod_ref, ng_ref, wup_ref, wdn_ref)


def _attn_out_mlp_kernel(x_ref, a_ref, wout_ref, mod_ref, ng_ref, wup_ref, wdn_ref, o_ref):
    x = x_ref[...] + mod_ref[2:3, :] * _dot(a_ref[...], wout_ref[...])
    o_ref[...] = _mlp_residual(x, mod_ref, ng_ref, wup_ref, wdn_ref)


def _mlp_layer(x, mod, layer, norm_g, w_up, w_down, attn=None, w_attn_out=None):
    bsz, seq, d = x.shape
    tm = TM_DENSE
    tile = pl.BlockSpec((None, tm, d), lambda b, i: (b, i, 0))
    operands, specs, body = [x], [tile], _mlp_kernel
    if attn is not None:
        operands += [attn, w_attn_out]
        specs += [tile, _resident((d, d))]
        body = _attn_out_mlp_kernel
    return pl.pallas_call(
        body,
        grid=(bsz, seq // tm),
        in_specs=specs + [
            pl.BlockSpec((None, None, 6, d), lambda b, i: (layer, b, 0, 0)),
            _resident((1, d)),
            _resident((d, D_FF)),
            _resident((D_FF, d)),
        ],
        out_specs=tile,
        out_shape=jax.ShapeDtypeStruct(x.shape, F32),
        compiler_params=_params(2),
        name="relu2_mlp",
    )(*operands, mod, norm_g, w_up, w_down)


def _head_rmsnorm(t, gain_tiled, out_scale):
    row = lax.broadcasted_iota(jnp.int32, (LANES, LANES), 0) // HEAD_DIM
    col = lax.broadcasted_iota(jnp.int32, (LANES, LANES), 1) // HEAD_DIM
    same_head = jnp.where(row == col, 1.0, 0.0).astype(BF16)
    pieces = []
    for j in range(t.shape[1] // LANES):
        tj = t[:, j * LANES:(j + 1) * LANES]
        ms = _dot_hi_lo(tj * tj, same_head) * (1.0 / HEAD_DIM)
        gj = gain_tiled[:, j * LANES:(j + 1) * LANES]
        pieces.append(tj * lax.rsqrt(ms + EPS) * (gj * out_scale))
    return pieces


def _qproj_kernel(x_ref, mod_ref, ng_ref, wq_ref, qg_ref, q_ref):
    h = _norm_modulate(x_ref[...], ng_ref[...], mod_ref[0:1, :], mod_ref[1:2, :])
    q = _dot(h.astype(BF16), wq_ref[...])
    for j, piece in enumerate(_head_rmsnorm(q, qg_ref[...], HEAD_DIM ** -0.5)):
        q_ref[:, j * LANES:(j + 1) * LANES] = piece.astype(BF16)


def _q_project(x, mod, layer, norm_g, w_q, q_gain_tiled):
    bsz, seq, d = x.shape
    tm = TM_DENSE
    return pl.pallas_call(
        _qproj_kernel,
        grid=(bsz, seq // tm),
        in_specs=[
            pl.BlockSpec((None, tm, d), lambda b, i: (b, i, 0)),
            pl.BlockSpec((None, None, 6, d), lambda b, i: (layer, b, 0, 0)),
            _resident((1, d)),
            _resident((d, d)),
            _resident((1, d)),
        ],
        out_specs=pl.BlockSpec((None, tm, d), lambda b, i: (b, i, 0)),
        out_shape=jax.ShapeDtypeStruct(x.shape, BF16),
        compiler_params=_params(2),
        name="q_project",
    )(x, mod, norm_g, w_q, q_gain_tiled)


def _kvproj_kernel(x_ref, mod_ref, ng_ref, wkv_ref, kg_ref, k_ref, v_ref):
    h = _norm_modulate(x_ref[...], ng_ref[...], mod_ref[0:1, :], mod_ref[1:2, :])
    kv = _dot(h.astype(BF16), wkv_ref[...])
    for j, piece in enumerate(_head_rmsnorm(kv[:, :D_MODEL], kg_ref[...], 1.0)):
        k_ref[:, j * LANES:(j + 1) * LANES] = piece.astype(BF16)
    v_ref[...] = kv[:, D_MODEL:].astype(BF16)


def _kv_project(x, kvmod, norm_g, w_kv, k_gain_tiled):
    bsz, seq, d = x.shape
    tm = TM_DENSE
    return pl.pallas_call(
        _kvproj_kernel,
        grid=(bsz, seq // tm),
        in_specs=[
            pl.BlockSpec((None, tm, d), lambda b, i: (b, i, 0)),
            pl.BlockSpec((None, None, 2, d), lambda b, i: (0, b, 0, 0)),
            _resident((1, d)),
            _resident((d, 2 * d)),
            _resident((1, d)),
        ],
        out_specs=[pl.BlockSpec((None, tm, d), lambda b, i: (b, i, 0))] * 2,
        out_shape=[jax.ShapeDtypeStruct(x.shape, BF16)] * 2,
        compiler_params=_params(2),
        name="kv_project",
    )(x, kvmod, norm_g, w_kv, k_gain_tiled)


def _attn_kernel(q_ref, k_ref, v_ref, o_ref):
    seq = q_ref.shape[0]
    lane_head = lax.broadcasted_iota(jnp.int32, (TQ, LANES), 1) // HEAD_DIM
    row = lax.broadcasted_iota(jnp.int32, (TQ, TK), 0)
    col = lax.broadcasted_iota(jnp.int32, (TQ, TK), 1)
    strictly_causal = col < row
    suffix = jnp.where(row >= col, 1.0, 0.0).astype(BF16)

    def pair_lanes(head):
        pair = head // HEADS_PER_BLOCK
        return slice(pair * LANES, (pair + 1) * LANES)

    def logits(qh, kb):
        return lax.dot_general(qh, kb, (((1,), (1,)), ((), ())),
                               preferred_element_type=F32)

    def keep_sums(z, diagonal):
        keep = jnp.maximum(z, 0.0) + jnp.log(1.0 + jnp.exp2(jnp.abs(z) * NEG_LOG2_E))
        if diagonal:
            keep = jnp.where(strictly_causal, keep, 0.0)
        return _dot(keep.astype(BF16), suffix), jnp.sum(keep, axis=-1, keepdims=True)

    def weights(z, incl, run, diagonal):
        w = jnp.exp(z - incl - run)
        if diagonal:
            w = jnp.where(strictly_causal, w, 0.0)
        return w.astype(BF16)

    def walk(q_heads, blocks, runs, accs):
        heads = range(ATTN_HEADS)
        zs = [[logits(q_heads[h], k_ref[ks, pair_lanes(h)]) for h in heads] for ks, _ in blocks]
        sums = [[keep_sums(zs[b][h], diagonal) for h in heads]
                for b, (_, diagonal) in enumerate(blocks)]
        runs, accs = list(runs), list(accs)
        for b, (ks, diagonal) in enumerate(blocks):
            for h in heads:
                incl, row_sum = sums[b][h]
                w = weights(zs[b][h], incl, runs[h], diagonal)
                accs[h] = accs[h] + _dot(w, v_ref[ks, pair_lanes(h)])
                runs[h] = runs[h] + row_sum
        return runs, accs

    def split_heads(qs):
        out = []
        for h in range(ATTN_HEADS):
            q = q_ref[qs, pair_lanes(h)]
            out.append(jnp.where(lane_head == h % HEADS_PER_BLOCK, q, jnp.zeros_like(q)))
        return out

    def min_run(runs):
        return functools.reduce(jnp.minimum, [jnp.min(r) for r in runs])

    def finish(qs, q_heads, runs, accs, next_block):
        def cond(state):
            return jnp.logical_and(state[0] >= 0, state[1] < -LOG_WEIGHT_FLOOR)

        def body(state):
            j = state[0]
            ks = pl.ds(pl.multiple_of(j * TK, TK), TK)
            runs, accs = walk(q_heads, [(ks, False)], state[2:2 + ATTN_HEADS],
                              state[2 + ATTN_HEADS:])
            return (j - 1, min_run(runs), *runs, *accs)

        state = lax.while_loop(cond, body, (next_block, min_run(runs), *runs, *accs))
        accs = state[2 + ATTN_HEADS:]
        for pair in range(ATTN_HEADS // HEADS_PER_BLOCK):
            out = jnp.where(lane_head == 0, accs[2 * pair], accs[2 * pair + 1])
            o_ref[qs, pair * LANES:(pair + 1) * LANES] = out.astype(o_ref.dtype)

    zero_runs = [jnp.zeros((TQ, 1), F32)] * ATTN_HEADS
    zero_accs = [jnp.zeros((TQ, LANES), F32)] * ATTN_HEADS

    first = pl.ds(0, TQ)
    q_heads = split_heads(first)
    runs, accs = walk(q_heads, [(first, True)], zero_runs, zero_accs)
    finish(first, q_heads, runs, accs, jnp.int32(-1))

    def q_block(qi, carry):
        qs = pl.ds(pl.multiple_of(qi * TQ, TQ), TQ)
        previous = pl.ds(pl.multiple_of((qi - 1) * TK, TK), TK)
        q_heads = split_heads(qs)
        runs, accs = walk(q_heads, [(qs, True), (previous, False)], zero_runs, zero_accs)
        finish(qs, q_heads, runs, accs, qi - 2)
        return carry

    lax.fori_loop(1, seq // TQ, q_block, 0)


def _stick_breaking_attention(q, k, v):
    bsz, seq, d = q.shape
    spec = pl.BlockSpec((None, seq, ATTN_WIDTH), lambda b, p: (b, 0, p))
    return pl.pallas_call(
        _attn_kernel,
        grid=(bsz, d // ATTN_WIDTH),
        in_specs=[spec, spec, spec],
        out_specs=spec,
        out_shape=jax.ShapeDtypeStruct(q.shape, BF16),
        compiler_params=_params(2),
        name="stick_breaking_attention",
    )(q, k, v)


def kernel(x, c, mod_w, mod_b, norm_g, mlp_w_up, mlp_w_down, a_w_in, a_v_norm_g,
           a_w_spatial, a_b_spatial, a_w_out, kv_mod_w, kv_mod_b, kv_norm_g, kv_w,
           k_norm_g, b_w_q, q_norm_g, b_w_out):
    d = x.shape[-1]
    mod = _mod_project(c, mod_w, mod_b)
    kvmod = _mod_project(c, kv_mod_w[None], kv_mod_b[None])
    k = v = None
    for layer in range(DEPTH):
        token_gain = norm_g[layer, 0].reshape(1, d)
        channel_gain = norm_g[layer, 1].reshape(1, d)
        attn = w_attn_out = None
        if layer < N_A_LAYERS:
            x = _gmlp_layer(
                x, mod, layer, token_gain, a_w_in[layer].astype(BF16),
                a_v_norm_g[layer].reshape(1, GMLP_WIDTH), a_w_spatial[layer],
                a_b_spatial[layer].T, a_w_out[layer].astype(BF16))
        else:
            j = layer - N_A_LAYERS
            q = _q_project(x, mod, layer, token_gain, b_w_q[j].astype(BF16),
                           jnp.tile(q_norm_g[j], N_HEADS).reshape(1, d))
            attn = _stick_breaking_attention(q, k, v)
            w_attn_out = b_w_out[j].astype(BF16)
        x = _mlp_layer(x, mod, layer, channel_gain, mlp_w_up[layer].astype(BF16),
                       mlp_w_down[layer].astype(BF16), attn, w_attn_out)
        if layer == N_A_LAYERS - 1:
            k, v = _kv_project(x, kvmod, kv_norm_g.reshape(1, d), kv_w.astype(BF16),
                               jnp.tile(k_norm_g, N_HEADS).reshape(1, d))
    return x
```

```python
import functools

import jax
import jax.numpy as jnp
from jax import lax
from jax.experimental import pallas as pl
from jax.experimental.pallas import tpu as pltpu

D_MODEL = 1024
DEPTH = 4
N_A_LAYERS = DEPTH // 2
CHUNK = 128
GMLP_WIDTH = 2 * D_MODEL
GMLP_GROUPS = 8
GROUP_WIDTH = GMLP_WIDTH // GMLP_GROUPS
N_HEADS = 16
HEAD_DIM = D_MODEL // N_HEADS
D_FF = 4 * D_MODEL
EPS = 1e-6

LANES = 128
HEADS_PER_BLOCK = LANES // HEAD_DIM
ATTN_HEADS = 4
ATTN_WIDTH = ATTN_HEADS * HEAD_DIM
VMEM_LIMIT_BYTES = 56 * 1024 * 1024

TM_DENSE = 1024
TM_PROJECT = 512
TQ = 256
TK = TQ
LOG_WEIGHT_FLOOR = -104.0
NEG_LOG2_E = -1.4426950408889634

F32 = jnp.float32
BF16 = jnp.bfloat16


def _dot(a, b):
    return jnp.dot(a, b, preferred_element_type=F32)


def _dot_hi_lo(a_f32, b_bf16):
    hi = a_f32.astype(BF16)
    lo = (a_f32 - hi.astype(F32)).astype(BF16)
    return _dot(hi, b_bf16) + _dot(lo, b_bf16)


def _norm_modulate(x, gain, shift, scale):
    ms = jnp.mean(x * x, axis=-1, keepdims=True)
    h = x * lax.rsqrt(ms + EPS) * gain
    return h * (1.0 + scale) + shift


def _resident(shape):
    zeros = (0,) * len(shape)
    return pl.BlockSpec(shape, lambda *_: zeros, pipeline_mode=pl.Buffered(1))


def _params(n_grid_axes):
    return pltpu.CompilerParams(
        dimension_semantics=("arbitrary",) * n_grid_axes,
        vmem_limit_bytes=VMEM_LIMIT_BYTES)


def _mod_kernel(c_ref, w_ref, b_ref, o_ref):
    o_ref[...] = _dot(c_ref[...].astype(BF16), w_ref[...].astype(BF16)) + b_ref[...]


def _mod_project(c, w, b):
    n_layers, d, nd = w.shape
    n_vec = nd // d
    bsz = c.shape[0]
    out = pl.pallas_call(
        _mod_kernel,
        grid=(n_layers, n_vec),
        in_specs=[
            pl.BlockSpec((bsz, d), lambda l, n: (0, 0)),
            pl.BlockSpec((None, d, d), lambda l, n: (l, 0, n)),
            pl.BlockSpec((None, 1, d), lambda l, n: (l, 0, n)),
        ],
        out_specs=pl.BlockSpec((None, None, bsz, d), lambda l, n: (l, n, 0, 0)),
        out_shape=jax.ShapeDtypeStruct((n_layers, n_vec, bsz, d), F32),
        compiler_params=_params(2),
        name="adaln_project",
    )(c, w, b.reshape(n_layers, 1, nd))
    return out.transpose(0, 2, 1, 3)


def _gmlp_kernel(x_ref, mod_ref, ng_ref, win_ref, vg_ref, ws_ref, bs_ref,
                 wout_ref, o_ref, gated_ref, *, tm):
    x = x_ref[...]
    h = _norm_modulate(x, ng_ref[...], mod_ref[0:1, :], mod_ref[1:2, :])
    uv = _dot(h.astype(BF16), win_ref[...])
    uv = 0.5 * uv * (1.0 + lax.erf(uv * (0.5 ** 0.5)))
    u = uv[:, :GMLP_WIDTH]
    v = uv[:, GMLP_WIDTH:]
    vms = jnp.mean(v * v, axis=-1, keepdims=True)
    vb = (v * lax.rsqrt(vms + EPS) * vg_ref[...]).astype(BF16)

    row = lax.broadcasted_iota(jnp.int32, (CHUNK, CHUNK), 0)
    col = lax.broadcasted_iota(jnp.int32, (CHUNK, CHUNK), 1)
    tril = col <= row
    for g in range(GMLP_GROUPS):
        w_causal = jnp.where(tril, ws_ref[g], 0.0).astype(BF16)
        bias = bs_ref[:, g:g + 1]
        cs = slice(g * GROUP_WIDTH, (g + 1) * GROUP_WIDTH)
        for n in range(tm // CHUNK):
            rs = slice(n * CHUNK, (n + 1) * CHUNK)
            z = _dot(w_causal, vb[rs, cs]) + bias
            gated_ref[rs, cs] = (u[rs, cs] * z).astype(BF16)
    y = _dot(gated_ref[...], wout_ref[...])
    o_ref[...] = x + mod_ref[2:3, :] * y


def _gmlp_layer(x, mod, layer, norm_g, w_in, v_norm_g, w_spatial, b_spatial_t, w_out):
    bsz, seq, d = x.shape
    tm = TM_DENSE
    return pl.pallas_call(
        functools.partial(_gmlp_kernel, tm=tm),
        grid=(bsz, seq // tm),
        in_specs=[
            pl.BlockSpec((None, tm, d), lambda b, i: (b, i, 0)),
            pl.BlockSpec((None, None, 6, d), lambda b, i: (layer, b, 0, 0)),
            _resident((1, d)),
            _resident((d, 2 * GMLP_WIDTH)),
            _resident((1, GMLP_WIDTH)),
            _resident((GMLP_GROUPS, CHUNK, CHUNK)),
            _resident((CHUNK, GMLP_GROUPS)),
            _resident((GMLP_WIDTH, d)),
        ],
        out_specs=pl.BlockSpec((None, tm, d), lambda b, i: (b, i, 0)),
        out_shape=jax.ShapeDtypeStruct(x.shape, F32),
        scratch_shapes=[pltpu.VMEM((tm, GMLP_WIDTH), BF16)],
        compiler_params=_params(2),
        name="gmlp_mixer",
    )(x, mod, norm_g, w_in, v_norm_g, w_spatial, b_spatial_t, w_out)


def _mlp_residual(x, mod_ref, ng_ref, wup_ref, wdn_ref):
    h = _norm_modulate(x, ng_ref[...], mod_ref[3:4, :], mod_ref[4:5, :])
    a = jnp.maximum(_dot(h.astype(BF16), wup_ref[...]), 0.0)
    y = _dot((a * a).astype(BF16), wdn_ref[...])
    return x + mod_ref[5:6, :] * y


def _mlp_kernel(x_ref, mod_ref, ng_ref, wup_ref, wdn_ref, o_ref):
    o_ref[...] = _mlp_residual(x_ref[...], mod_ref, ng_ref, wup_ref, wdn_ref)


def _attn_out_mlp_kernel(x_ref, a_ref, wout_ref, mod_ref, ng_ref, wup_ref, wdn_ref, o_ref):
    x = x_ref[...] + mod_ref[2:3, :] * _dot(a_ref[...], wout_ref[...])
    o_ref[...] = _mlp_residual(x, mod_ref, ng_ref, wup_ref, wdn_ref)


def _mlp_layer(x, mod, layer, norm_g, w_up, w_down, attn=None, w_attn_out=None):
    bsz, seq, d = x.shape
    tm = TM_DENSE
    tile = pl.BlockSpec((None, tm, d), lambda b, i: (b, i, 0))
    operands, specs, body = [x], [tile], _mlp_kernel
    if attn is not None:
        operands += [attn, w_attn_out]
        specs += [tile, _resident((d, d))]
        body = _attn_out_mlp_kernel
    return pl.pallas_call(
        body,
        grid=(bsz, seq // tm),
        in_specs=specs + [
            pl.BlockSpec((None, None, 6, d), lambda b, i: (layer, b, 0, 0)),
            _resident((1, d)),
            _resident((d, D_FF)),
            _resident((D_FF, d)),
        ],
        out_specs=tile,
        out_shape=jax.ShapeDtypeStruct(x.shape, F32),
        compiler_params=_params(2),
        name="relu2_mlp",
    )(*operands, mod, norm_g, w_up, w_down)


def _head_rmsnorm(t, gain_tiled, out_scale):
    row = lax.broadcasted_iota(jnp.int32, (LANES, LANES), 0) // HEAD_DIM
    col = lax.broadcasted_iota(jnp.int32, (LANES, LANES), 1) // HEAD_DIM
    same_head = jnp.where(row == col, 1.0, 0.0).astype(BF16)
    pieces = []
    for j in range(t.shape[1] // LANES):
        tj = t[:, j * LANES:(j + 1) * LANES]
        ms = _dot_hi_lo(tj * tj, same_head) * (1.0 / HEAD_DIM)
        gj = gain_tiled[:, j * LANES:(j + 1) * LANES]
        pieces.append(tj * lax.rsqrt(ms + EPS) * (gj * out_scale))
    return pieces


def _qproj_kernel(x_ref, mod_ref, ng_ref, wq_ref, qg_ref, q_ref):
    h = _norm_modulate(x_ref[...], ng_ref[...], mod_ref[0:1, :], mod_ref[1:2, :])
    q = _dot(h.astype(BF16), wq_ref[...])
    for j, piece in enumerate(_head_rmsnorm(q, qg_ref[...], HEAD_DIM ** -0.5)):
        q_ref[:, j * LANES:(j + 1) * LANES] = piece.astype(BF16)


def _q_project(x, mod, layer, norm_g, w_q, q_gain_tiled):
    bsz, seq, d = x.shape
    tm = TM_PROJECT
    return pl.pallas_call(
        _qproj_kernel,
        grid=(bsz, seq // tm),
        in_specs=[
            pl.BlockSpec((None, tm, d), lambda b, i: (b, i, 0)),
            pl.BlockSpec((None, None, 6, d), lambda b, i: (layer, b, 0, 0)),
            _resident((1, d)),
            _resident((d, d)),
            _resident((1, d)),
        ],
        out_specs=pl.BlockSpec((None, tm, d), lambda b, i: (b, i, 0)),
        out_shape=jax.ShapeDtypeStruct(x.shape, BF16),
        compiler_params=_params(2),
        name="q_project",
    )(x, mod, norm_g, w_q, q_gain_tiled)


def _kvproj_kernel(x_ref, mod_ref, ng_ref, wkv_ref, kg_ref, k_ref, v_ref):
    h = _norm_modulate(x_ref[...], ng_ref[...], mod_ref[0:1, :], mod_ref[1:2, :])
    kv = _dot(h.astype(BF16), wkv_ref[...])
    for j, piece in enumerate(_head_rmsnorm(kv[:, :D_MODEL], kg_ref[...], 1.0)):
        k_ref[:, j * LANES:(j + 1) * LANES] = piece.astype(BF16)
    v_ref[...] = kv[:, D_MODEL:].astype(BF16)


def _kv_project(x, kvmod, norm_g, w_kv, k_gain_tiled):
    bsz, seq, d = x.shape
    tm = TM_PROJECT
    return pl.pallas_call(
        _kvproj_kernel,
        grid=(bsz, seq // tm),
        in_specs=[
            pl.BlockSpec((None, tm, d), lambda b, i: (b, i, 0)),
            pl.BlockSpec((None, None, 2, d), lambda b, i: (0, b, 0, 0)),
            _resident((1, d)),
            _resident((d, 2 * d)),
            _resident((1, d)),
        ],
        out_specs=[pl.BlockSpec((None, tm, d), lambda b, i: (b, i, 0))] * 2,
        out_shape=[jax.ShapeDtypeStruct(x.shape, BF16)] * 2,
        compiler_params=_params(2),
        name="kv_project",
    )(x, kvmod, norm_g, w_kv, k_gain_tiled)


def _attn_kernel(q_ref, k_ref, v_ref, o_ref):
    seq = q_ref.shape[0]
    lane_head = lax.broadcasted_iota(jnp.int32, (TQ, LANES), 1) // HEAD_DIM
    row = lax.broadcasted_iota(jnp.int32, (TQ, TK), 0)
    col = lax.broadcasted_iota(jnp.int32, (TQ, TK), 1)
    strictly_causal = col < row
    suffix = jnp.where(row >= col, 1.0, 0.0).astype(BF16)

    def pair_lanes(head):
        pair = head // HEADS_PER_BLOCK
        return slice(pair * LANES, (pair + 1) * LANES)

    def logits(qh, kb):
        return lax.dot_general(qh, kb, (((1,), (1,)), ((), ())),
                               preferred_element_type=F32)

    def keep_sums(z, diagonal):
        keep = jnp.maximum(z, 0.0) + jnp.log(1.0 + jnp.exp2(jnp.abs(z) * NEG_LOG2_E))
        if diagonal:
            keep = jnp.where(strictly_causal, keep, 0.0)
        return _dot(keep.astype(BF16), suffix), jnp.sum(keep, axis=-1, keepdims=True)

    def weights(z, incl, run, diagonal):
        w = jnp.exp(z - incl - run)
        if diagonal:
            w = jnp.where(strictly_causal, w, 0.0)
        return w.astype(BF16)

    def walk(q_heads, blocks, runs, accs):
        heads = range(ATTN_HEADS)
        zs = [[logits(q_heads[h], k_ref[ks, pair_lanes(h)]) for h in heads] for ks, _ in blocks]
        sums = [[keep_sums(zs[b][h], diagonal) for h in heads]
                for b, (_, diagonal) in enumerate(blocks)]
        runs, accs = list(runs), list(accs)
        for b, (ks, diagonal) in enumerate(blocks):
            for h in heads:
                incl, row_sum = sums[b][h]
                w = weights(zs[b][h], incl, runs[h], diagonal)
                accs[h] = accs[h] + _dot(w, v_ref[ks, pair_lanes(h)])
                runs[h] = runs[h] + row_sum
        return runs, accs

    def split_heads(qs):
        out = []
        for h in range(ATTN_HEADS):
            q = q_ref[qs, pair_lanes(h)]
            out.append(jnp.where(lane_head == h % HEADS_PER_BLOCK, q, jnp.zeros_like(q)))
        return out

    def min_run(runs):
        return functools.reduce(jnp.minimum, [jnp.min(r) for r in runs])

    def finish(qs, q_heads, runs, accs, next_block):
        def cond(state):
            return jnp.logical_and(state[0] >= 0, state[1] < -LOG_WEIGHT_FLOOR)

        def body(state):
            j = state[0]
            ks = pl.ds(pl.multiple_of(j * TK, TK), TK)
            runs, accs = walk(q_heads, [(ks, False)], state[2:2 + ATTN_HEADS],
                              state[2 + ATTN_HEADS:])
            return (j - 1, min_run(runs), *runs, *accs)

        state = lax.while_loop(cond, body, (next_block, min_run(runs), *runs, *accs))
        accs = state[2 + ATTN_HEADS:]
        for pair in range(ATTN_HEADS // HEADS_PER_BLOCK):
            out = jnp.where(lane_head == 0, accs[2 * pair], accs[2 * pair + 1])
            o_ref[qs, pair * LANES:(pair + 1) * LANES] = out.astype(o_ref.dtype)

    zero_runs = [jnp.zeros((TQ, 1), F32)] * ATTN_HEADS
    zero_accs = [jnp.zeros((TQ, LANES), F32)] * ATTN_HEADS

    first = pl.ds(0, TQ)
    q_heads = split_heads(first)
    runs, accs = walk(q_heads, [(first, True)], zero_runs, zero_accs)
    finish(first, q_heads, runs, accs, jnp.int32(-1))

    def q_block(qi, carry):
        qs = pl.ds(pl.multiple_of(qi * TQ, TQ), TQ)
        previous = pl.ds(pl.multiple_of((qi - 1) * TK, TK), TK)
        q_heads = split_heads(qs)
        runs, accs = walk(q_heads, [(qs, True), (previous, False)], zero_runs, zero_accs)
        finish(qs, q_heads, runs, accs, qi - 2)
        return carry

    lax.fori_loop(1, seq // TQ, q_block, 0)


def _stick_breaking_attention(q, k, v):
    bsz, seq, d = q.shape
    spec = pl.BlockSpec((None, seq, ATTN_WIDTH), lambda b, p: (b, 0, p))
    return pl.pallas_call(
        _attn_kernel,
        grid=(bsz, d // ATTN_WIDTH),
        in_specs=[spec, spec, spec],
        out_specs=spec,
        out_shape=jax.ShapeDtypeStruct(q.shape, BF16),
        compiler_params=_params(2),
        name="stick_breaking_attention",
    )(q, k, v)


def kernel(x, c, mod_w, mod_b, norm_g, mlp_w_up, mlp_w_down, a_w_in, a_v_norm_g,
           a_w_spatial, a_b_spatial, a_w_out, kv_mod_w, kv_mod_b, kv_norm_g, kv_w,
           k_norm_g, b_w_q, q_norm_g, b_w_out):
    d = x.shape[-1]
    mod = _mod_project(c, mod_w, mod_b)
    kvmod = _mod_project(c, kv_mod_w[None], kv_mod_b[None])
    k = v = None
    for layer in range(DEPTH):
        token_gain = norm_g[layer, 0].reshape(1, d)
        channel_gain = norm_g[layer, 1].reshape(1, d)
        attn = w_attn_out = None
        if layer < N_A_LAYERS:
            x = _gmlp_layer(
                x, mod, layer, token_gain, a_w_in[layer].astype(BF16),
                a_v_norm_g[layer].reshape(1, GMLP_WIDTH), a_w_spatial[layer],
                a_b_spatial[layer].T, a_w_out[layer].astype(BF16))
        else:
            j = layer - N_A_LAYERS
            q = _q_project(x, mod, layer, token_gain, b_w_q[j].astype(BF16),
                           jnp.tile(q_norm_g[j], N_HEADS).reshape(1, d))
            attn = _stick_breaking_attention(q, k, v)
            w_attn_out = b_w_out[j].astype(BF16)
        x = _mlp_layer(x, mod, layer, channel_gain, mlp_w_up[layer].astype(BF16),
                       mlp_w_down[layer].astype(BF16), attn, w_attn_out)
        if layer == N_A_LAYERS - 1:
            k, v = _kv_project(x, kvmod, kv_norm_g.reshape(1, d), kv_w.astype(BF16),
                               jnp.tile(k_norm_g, N_HEADS).reshape(1, d))
    return x
```

```python
import functools

import jax
import jax.numpy as jnp
from jax import lax
from jax.experimental import pallas as pl
from jax.experimental.pallas import tpu as pltpu

D_MODEL = 1024
DEPTH = 4
N_A_LAYERS = DEPTH // 2
CHUNK = 128
GMLP_WIDTH = 2 * D_MODEL
GMLP_GROUPS = 8
GROUP_WIDTH = GMLP_WIDTH // GMLP_GROUPS
N_HEADS = 16
HEAD_DIM = D_MODEL // N_HEADS
D_FF = 4 * D_MODEL
EPS = 1e-6

LANES = 128
HEADS_PER_BLOCK = LANES // HEAD_DIM
ATTN_HEADS = 4
ATTN_WIDTH = ATTN_HEADS * HEAD_DIM
VMEM_LIMIT_BYTES = 56 * 1024 * 1024

TM_DENSE = 1024
TM_PROJECT = 512
TQ = 256
TK = TQ
LOG_WEIGHT_FLOOR = -104.0
NEG_LOG2_E = -1.4426950408889634

F32 = jnp.float32
BF16 = jnp.bfloat16


def _dot(a, b):
    return jnp.dot(a, b, preferred_element_type=F32)


def _dot_hi_lo(a_f32, b_bf16):
    hi = a_f32.astype(BF16)
    lo = (a_f32 - hi.astype(F32)).astype(BF16)
    return _dot(hi, b_bf16) + _dot(lo, b_bf16)


def _norm_modulate(x, gain, shift, scale):
    ms = jnp.mean(x * x, axis=-1, keepdims=True)
    h = x * lax.rsqrt(ms + EPS) * gain
    return h * (1.0 + scale) + shift


def _resident(shape):
    zeros = (0,) * len(shape)
    return pl.BlockSpec(shape, lambda *_: zeros, pipeline_mode=pl.Buffered(1))


def _params(n_grid_axes):
    return pltpu.CompilerParams(
        dimension_semantics=("arbitrary",) * n_grid_axes,
        vmem_limit_bytes=VMEM_LIMIT_BYTES)


def _mod_kernel(c_ref, w_ref, b_ref, o_ref):
    o_ref[...] = _dot(c_ref[...].astype(BF16), w_ref[...].astype(BF16)) + b_ref[...]


def _mod_project(c, w, b):
    n_layers, d, nd = w.shape
    n_vec = nd // d
    bsz = c.shape[0]
    out = pl.pallas_call(
        _mod_kernel,
        grid=(n_layers, n_vec),
        in_specs=[
            pl.BlockSpec((bsz, d), lambda l, n: (0, 0)),
            pl.BlockSpec((None, d, d), lambda l, n: (l, 0, n)),
            pl.BlockSpec((None, 1, d), lambda l, n: (l, 0, n)),
        ],
        out_specs=pl.BlockSpec((None, None, bsz, d), lambda l, n: (l, n, 0, 0)),
        out_shape=jax.ShapeDtypeStruct((n_layers, n_vec, bsz, d), F32),
        compiler_params=_params(2),
        name="adaln_project",
    )(c, w, b.reshape(n_layers, 1, nd))
    return out.transpose(0, 2, 1, 3)


def _gmlp_kernel(x_ref, mod_ref, ng_ref, win_ref, vg_ref, ws_ref, bs_ref,
                 wout_ref, o_ref, gated_ref, *, tm):
    x = x_ref[...]
    h = _norm_modulate(x, ng_ref[...], mod_ref[0:1, :], mod_ref[1:2, :])
    uv = _dot(h.astype(BF16), win_ref[...])
    uv = 0.5 * uv * (1.0 + lax.erf(uv * (0.5 ** 0.5)))
    u = uv[:, :GMLP_WIDTH]
    v = uv[:, GMLP_WIDTH:]
    vms = jnp.mean(v * v, axis=-1, keepdims=True)
    vb = (v * lax.rsqrt(vms + EPS) * vg_ref[...]).astype(BF16)

    row = lax.broadcasted_iota(jnp.int32, (CHUNK, CHUNK), 0)
    col = lax.broadcasted_iota(jnp.int32, (CHUNK, CHUNK), 1)
    tril = col <= row
    for g in range(GMLP_GROUPS):
        w_causal = jnp.where(tril, ws_ref[g], 0.0).astype(BF16)
        bias = bs_ref[:, g:g + 1]
        cs = slice(g * GROUP_WIDTH, (g + 1) * GROUP_WIDTH)
        for n in range(tm // CHUNK):
            rs = slice(n * CHUNK, (n + 1) * CHUNK)
            z = _dot(w_causal, vb[rs, cs]) + bias
            gated_ref[rs, cs] = (u[rs, cs] * z).astype(BF16)
    y = _dot(gated_ref[...], wout_ref[...])
    o_ref[...] = x + mod_ref[2:3, :] * y


def _gmlp_layer(x, mod, layer, norm_g, w_in, v_norm_g, w_spatial, b_spatial_t, w_out):
    bsz, seq, d = x.shape
    tm = TM_DENSE
    return pl.pallas_call(
        functools.partial(_gmlp_kernel, tm=tm),
        grid=(bsz, seq // tm),
        in_specs=[
            pl.BlockSpec((None, tm, d), lambda b, i: (b, i, 0)),
            pl.BlockSpec((None, None, 6, d), lambda b, i: (layer, b, 0, 0)),
            _resident((1, d)),
            _resident((d, 2 * GMLP_WIDTH)),
            _resident((1, GMLP_WIDTH)),
            _resident((GMLP_GROUPS, CHUNK, CHUNK)),
            _resident((CHUNK, GMLP_GROUPS)),
            _resident((GMLP_WIDTH, d)),
        ],
        out_specs=pl.BlockSpec((None, tm, d), lambda b, i: (b, i, 0)),
        out_shape=jax.ShapeDtypeStruct(x.shape, F32),
        scratch_shapes=[pltpu.VMEM((tm, GMLP_WIDTH), BF16)],
        compiler_params=_params(2),
        name="gmlp_mixer",
    )(x, mod, norm_g, w_in, v_norm_g, w_spatial, b_spatial_t, w_out)


def _mlp_residual(x, mod_ref, ng_ref, wup_ref, wdn_ref):
    h = _norm_modulate(x, ng_ref[...], mod_ref[3:4, :], mod_ref[4:5, :])
    a = jnp.maximum(_dot(h.astype(BF16), wup_ref[...]), 0.0)
    y = _dot((a * a).astype(BF16), wdn_ref[...])
    return x + mod_ref[5:6, :] * y


def _mlp_kernel(x_ref, mod_ref, ng_ref, wup_ref, wdn_ref, o_ref):
    o_ref[...] = _mlp_residual(x_ref[...], mod_ref, ng_ref, wup_ref, wdn_ref)


def _attn_out_mlp_kernel(x_ref, a_ref, wout_ref, mod_ref, ng_ref, wup_ref, wdn_ref, o_ref):
    x = x_ref[...] + mod_ref[2:3, :] * _dot(a_ref[...], wout_ref[...])
    o_ref[...] = _mlp_residual(x, mod_ref, ng_ref, wup_ref, wdn_ref)


def _mlp_layer(x, mod, layer, norm_g, w_up, w_down, attn=None, w_attn_out=None):
    bsz, seq, d = x.shape
    tm = TM_DENSE
    tile = pl.BlockSpec((None, tm, d), lambda b, i: (b, i, 0))
    operands, specs, body = [x], [tile], _mlp_kernel
    if attn is not None:
        operands += [attn, w_attn_out]
        specs += [tile, _resident((d, d))]
        body = _attn_out_mlp_kernel
    return pl.pallas_call(
        body,
        grid=(bsz, seq // tm),
        in_specs=specs + [
            pl.BlockSpec((None, None, 6, d), lambda b, i: (layer, b, 0, 0)),
            _resident((1, d)),
            _resident((d, D_FF)),
            _resident((D_FF, d)),
        ],
        out_specs=tile,
        out_shape=jax.ShapeDtypeStruct(x.shape, F32),
        compiler_params=_params(2),
        name="relu2_mlp",
    )(*operands, mod, norm_g, w_up, w_down)


def _head_rmsnorm(t, gain_tiled, out_scale):
    row = lax.broadcasted_iota(jnp.int32, (LANES, LANES), 0) // HEAD_DIM
    col = lax.broadcasted_iota(jnp.int32, (LANES, LANES), 1) // HEAD_DIM
    same_head = jnp.where(row == col, 1.0, 0.0).astype(BF16)
    pieces = []
    for j in range(t.shape[1] // LANES):
        tj = t[:, j * LANES:(j + 1) * LANES]
        ms = _dot_hi_lo(tj * tj, same_head) * (1.0 / HEAD_DIM)
        gj = gain_tiled[:, j * LANES:(j + 1) * LANES]
        pieces.append(tj * lax.rsqrt(ms + EPS) * (gj * out_scale))
    return pieces


def _qproj_kernel(x_ref, mod_ref, ng_ref, wq_ref, qg_ref, q_ref):
    h = _norm_modulate(x_ref[...], ng_ref[...], mod_ref[0:1, :], mod_ref[1:2, :])
    q = _dot(h.astype(BF16), wq_ref[...])
    for j, piece in enumerate(_head_rmsnorm(q, qg_ref[...], HEAD_DIM ** -0.5)):
        q_ref[:, j * LANES:(j + 1) * LANES] = piece.astype(BF16)


def _q_project(x, mod, layer, norm_g, w_q, q_gain_tiled):
    bsz, seq, d = x.shape
    tm = TM_PROJECT
    return pl.pallas_call(
        _qproj_kernel,
        grid=(bsz, seq // tm),
        in_specs=[
            pl.BlockSpec((None, tm, d), lambda b, i: (b, i, 0)),
            pl.BlockSpec((None, None, 6, d), lambda b, i: (layer, b, 0, 0)),
            _resident((1, d)),
            _resident((d, d)),
            _resident((1, d)),
        ],
        out_specs=pl.BlockSpec((None, tm, d), lambda b, i: (b, i, 0)),
        out_shape=jax.ShapeDtypeStruct(x.shape, BF16),
        compiler_params=_params(2),
        name="q_project",
    )(x, mod, norm_g, w_q, q_gain_tiled)


def _kvproj_kernel(x_ref, mod_ref, ng_ref, wkv_ref, kg_ref, k_ref, v_ref):
    h = _norm_modulate(x_ref[...], ng_ref[...], mod_ref[0:1, :], mod_ref[1:2, :])
    kv = _dot(h.astype(BF16), wkv_ref[...])
    for j, piece in enumerate(_head_rmsnorm(kv[:, :D_MODEL], kg_ref[...], 1.0)):
        k_ref[:, j * LANES:(j + 1) * LANES] = piece.astype(BF16)
    v_ref[...] = kv[:, D_MODEL:].astype(BF16)


def _kv_project(x, kvmod, norm_g, w_kv, k_gain_tiled):
    bsz, seq, d = x.shape
    tm = TM_PROJECT
    return pl.pallas_call(
        _kvproj_kernel,
        grid=(bsz, seq // tm),
        in_specs=[
            pl.BlockSpec((None, tm, d), lambda b, i: (b, i, 0)),
            pl.BlockSpec((None, None, 2, d), lambda b, i: (0, b, 0, 0)),
            _resident((1, d)),
            _resident((d, 2 * d)),
            _resident((1, d)),
        ],
        out_specs=[pl.BlockSpec((None, tm, d), lambda b, i: (b, i, 0))] * 2,
        out_shape=[jax.ShapeDtypeStruct(x.shape, BF16)] * 2,
        compiler_params=_params(2),
        name="kv_project",
    )(x, kvmod, norm_g, w_kv, k_gain_tiled)


def _attn_kernel(q_ref, k_ref, v_ref, o_ref):
    seq = q_ref.shape[0]
    lane_head = lax.broadcasted_iota(jnp.int32, (TQ, LANES), 1) // HEAD_DIM
    row = lax.broadcasted_iota(jnp.int32, (TQ, TK), 0)
    col = lax.broadcasted_iota(jnp.int32, (TQ, TK), 1)
    strictly_causal = col < row
    suffix = jnp.where(row >= col, 1.0, 0.0).astype(BF16)

    def pair_lanes(head):
        pair = head // HEADS_PER_BLOCK
        return slice(pair * LANES, (pair + 1) * LANES)

    def logits(qh, kb):
        return lax.dot_general(qh, kb, (((1,), (1,)), ((), ())),
                               preferred_element_type=F32)

    def keep_sums(z, diagonal):
        keep = jnp.maximum(z, 0.0) + jnp.log(1.0 + jnp.exp2(jnp.abs(z) * NEG_LOG2_E))
        if diagonal:
            keep = jnp.where(strictly_causal, keep, 0.0)
        return _dot(keep.astype(BF16), suffix), jnp.sum(keep, axis=-1, keepdims=True)

    def weights(z, incl, run, diagonal):
        w = jnp.exp(z - incl - run)
        if diagonal:
            w = jnp.where(strictly_causal, w, 0.0)
        return w.astype(BF16)

    def walk(q_heads, blocks, runs, accs):
        heads = range(ATTN_HEADS)
        zs = [[logits(q_heads[h], k_ref[ks, pair_lanes(h)]) for h in heads] for ks, _ in blocks]
        sums = [[keep_sums(zs[b][h], diagonal) for h in heads]
                for b, (_, diagonal) in enumerate(blocks)]
        runs, accs = list(runs), list(accs)
        for b, (ks, diagonal) in enumerate(blocks):
            for h in heads:
                incl, row_sum = sums[b][h]
                w = weights(zs[b][h], incl, runs[h], diagonal)
                accs[h] = accs[h] + _dot(w, v_ref[ks, pair_lanes(h)])
                runs[h] = runs[h] + row_sum
        return runs, accs

    def split_heads(qs):
        out = []
        for h in range(ATTN_HEADS):
            q = q_ref[qs, pair_lanes(h)]
            out.append(jnp.where(lane_head == h % HEADS_PER_BLOCK, q, jnp.zeros_like(q)))
        return out

    def min_run(runs):
        return functools.reduce(jnp.minimum, [jnp.min(r) for r in runs])

    def finish(row_start, q_heads, runs, accs, next_block):
        half = TQ // 2

        def trips(q_rows, runs, accs, next_block, active):
            def cond(state):
                return jnp.logical_and(state[0] >= 0, state[1] < -LOG_WEIGHT_FLOOR)

            def body(state):
                j = state[0]
                ks = pl.ds(pl.multiple_of(j * TK, TK), TK)
                runs, accs = walk(q_rows, [(ks, False)], state[2:2 + ATTN_HEADS],
                                  state[2 + ATTN_HEADS:])
                return (j - 1, min_run([active(r) for r in runs]), *runs, *accs)

            state = lax.while_loop(
                cond, body, (next_block, min_run([active(r) for r in runs]), *runs, *accs))
            return state[0], state[2:2 + ATTN_HEADS], state[2 + ATTN_HEADS:]

        first_head = lax.broadcasted_iota(jnp.int32, (half, LANES), 1) < HEAD_DIM

        def store(rows, accs):
            for pair in range(ATTN_HEADS // HEADS_PER_BLOCK):
                out = jnp.where(first_head, accs[2 * pair], accs[2 * pair + 1])
                o_ref[rows, pair * LANES:(pair + 1) * LANES] = out.astype(o_ref.dtype)

        next_block, runs, accs = trips(q_heads, runs, accs, next_block, lambda r: r[half:])
        store(pl.ds(row_start + half, half), [a[half:] for a in accs])
        _, _, upper = trips([q[:half] for q in q_heads], [r[:half] for r in runs],
                            [a[:half] for a in accs], next_block, lambda r: r)
        store(pl.ds(row_start, half), upper)

    zero_runs = [jnp.zeros((TQ, 1), F32)] * ATTN_HEADS
    zero_accs = [jnp.zeros((TQ, LANES), F32)] * ATTN_HEADS

    first = pl.ds(0, TQ)
    q_heads = split_heads(first)
    runs, accs = walk(q_heads, [(first, True)], zero_runs, zero_accs)
    finish(0, q_heads, runs, accs, jnp.int32(-1))

    def q_block(qi, carry):
        qs = pl.ds(pl.multiple_of(qi * TQ, TQ), TQ)
        previous = pl.ds(pl.multiple_of((qi - 1) * TK, TK), TK)
        q_heads = split_heads(qs)
        runs, accs = walk(q_heads, [(qs, True), (previous, False)], zero_runs, zero_accs)
        finish(pl.multiple_of(qi * TQ, TQ), q_heads, runs, accs, qi - 2)
        return carry

    lax.fori_loop(1, seq // TQ, q_block, 0)


def _stick_breaking_attention(q, k, v):
    bsz, seq, d = q.shape
    spec = pl.BlockSpec((None, seq, ATTN_WIDTH), lambda b, p: (b, 0, p))
    return pl.pallas_call(
        _attn_kernel,
        grid=(bsz, d // ATTN_WIDTH),
        in_specs=[spec, spec, spec],
        out_specs=spec,
        out_shape=jax.ShapeDtypeStruct(q.shape, BF16),
        compiler_params=_params(2),
        name="stick_breaking_attention",
    )(q, k, v)


def kernel(x, c, mod_w, mod_b, norm_g, mlp_w_up, mlp_w_down, a_w_in, a_v_norm_g,
           a_w_spatial, a_b_spatial, a_w_out, kv_mod_w, kv_mod_b, kv_norm_g, kv_w,
           k_norm_g, b_w_q, q_norm_g, b_w_out):
    d = x.shape[-1]
    mod = _mod_project(c, mod_w, mod_b)
    kvmod = _mod_project(c, kv_mod_w[None], kv_mod_b[None])
    k = v = None
    for layer in range(DEPTH):
        token_gain = norm_g[layer, 0].reshape(1, d)
        channel_gain = norm_g[layer, 1].reshape(1, d)
        attn = w_attn_out = None
        if layer < N_A_LAYERS:
            x = _gmlp_layer(
                x, mod, layer, token_gain, a_w_in[layer].astype(BF16),
                a_v_norm_g[layer].reshape(1, GMLP_WIDTH), a_w_spatial[layer],
                a_b_spatial[layer].T, a_w_out[layer].astype(BF16))
        else:
            j = layer - N_A_LAYERS
            q = _q_project(x, mod, layer, token_gain, b_w_q[j].astype(BF16),
                           jnp.tile(q_norm_g[j], N_HEADS).reshape(1, d))
            attn = _stick_breaking_attention(q, k, v)
            w_attn_out = b_w_out[j].astype(BF16)
        x = _mlp_layer(x, mod, layer, channel_gain, mlp_w_up[layer].astype(BF16),
                       mlp_w_down[layer].astype(BF16), attn, w_attn_out)
        if layer == N_A_LAYERS - 1:
            k, v = _kv_project(x, kvmod, kv_norm_g.reshape(1, d), kv_w.astype(BF16),
                               jnp.tile(k_norm_g, N_HEADS).reshape(1, d))
    return x
```
